```python
import math
import jax, jax.numpy as jnp
from jax import lax
import numpy as np

D_MODEL = 1024
BATCH = 8
SEQ = 2048
DEPTH = 4
DEC_BATCH = 128
DEC_SEQ = 8
PAST_LEN = 2048
PAGE_SIZE = 128

N_MIXERS = 2
N_A_LAYERS = (DEPTH + 1) // 2
N_B_LAYERS = DEPTH // 2
W_A = D_MODEL
LRU_BLOCKS = 8
LRU_BLOCK_DIM = W_A // LRU_BLOCKS
CONV_WIDTH = 4
LRU_C = 8.0
B_HEADS = 8
B_HEAD_DIM = D_MODEL // B_HEADS
W_B = B_HEADS * B_HEAD_DIM
MOBA_BLOCK = 256
MOBA_TOPK = 3
MOBA_Q_CHUNK = 16
REL_BUCKETS = 32
REL_MAX_DIST = 128
X_HEADS = 4
X_HEAD_DIM = 128
W_X = X_HEADS * X_HEAD_DIM
MEM_LEN = 256
D_FF = 4 * D_MODEL
EPS = 1e-6

kernel_name = 'hawk_moba_memory_decoder_step'


def rmsnorm(x, g):
    x32 = x.astype(jnp.float32)
    y = x32 * lax.rsqrt(jnp.mean(x32 * x32, axis=-1, keepdims=True) + EPS)
    return (y * g.astype(jnp.float32)).astype(x.dtype)


def sq_relu_mlp(x, w_up, w_down):
    return jnp.square(jax.nn.relu(x @ w_up)) @ w_down


def mem_kv(mem, g, w):
    kv = rmsnorm(mem, g) @ w
    k, v = jnp.split(kv, 2, axis=-1)
    b = mem.shape[0]
    return (k.reshape(b, MEM_LEN, X_HEADS, X_HEAD_DIM), v.reshape(b, MEM_LEN, X_HEADS, X_HEAD_DIM))


def cross_attend(q, mk, mv):
    b, t, _ = q.shape
    q = q.reshape(b, t, X_HEADS, X_HEAD_DIM)
    s = jnp.einsum('bthd,bmhd->bhtm', q, mk).astype(jnp.float32) * (X_HEAD_DIM ** -0.5)
    p = jax.nn.softmax(s, axis=-1).astype(mv.dtype)
    return jnp.einsum('bhtm,bmhd->bthd', p, mv).reshape(b, t, W_X)


def t5_bucket(dist):
    max_exact = REL_BUCKETS // 2
    large = max_exact + (jnp.log(jnp.maximum(dist, 1).astype(jnp.float32) / max_exact)
                         / math.log(REL_MAX_DIST / max_exact) * (REL_BUCKETS - max_exact)).astype(jnp.int32)
    large = jnp.minimum(large, REL_BUCKETS - 1)
    return jnp.where(dist < max_exact, dist, large)


def rglru(u, conv_buf, h0, pos0, conv_w, conv_b, ga_w, ga_b, gx_w, gx_b, lam):
    b, t, w = u.shape
    xpad = jnp.concatenate([conv_buf, u], axis=1)
    xc = conv_b + sum(conv_w[j] * xpad[:, j:j + t] for j in range(CONV_WIDTH))
    new_buf = xpad[:, t:]
    xb = xc.reshape(b, t, LRU_BLOCKS, LRU_BLOCK_DIM)
    r = jax.nn.sigmoid((jnp.einsum('btnd,nde->btne', xb, ga_w).reshape(b, t, w) + ga_b).astype(jnp.float32))
    ig = jax.nn.sigmoid((jnp.einsum('btnd,nde->btne', xb, gx_w).reshape(b, t, w) + gx_b).astype(jnp.float32))
    log_a = -LRU_C * r * jax.nn.softplus(-lam.astype(jnp.float32))
    a = jnp.exp(log_a)
    pos = pos0 + jnp.arange(t)
    mult = jnp.where((pos == 0)[None, :, None], 1.0, jnp.sqrt(-jnp.expm1(2.0 * log_a)))
    bx = mult * ig * xc.astype(jnp.float32)
    bx = bx.at[:, 0].add(a[:, 0] * h0.astype(jnp.float32))

    def combine(e1, e2):
        a1, b1 = e1
        a2, b2 = e2
        return (a1 * a2, a2 * b1 + b2)

    _, h = lax.associative_scan(combine, (a, bx), axis=1)
    return h.astype(u.dtype), new_buf, h[:, -1].astype(u.dtype)


def moba_attention(q, k_all, v_all, pos0, rel_bias, q_chunk):
    b, tq, h, dh = q.shape
    nb = k_all.shape[1] // MOBA_BLOCK
    kb = k_all.reshape(b, nb, MOBA_BLOCK, h, dh)
    vb = v_all.reshape(b, nb, MOBA_BLOCK, h, dh)
    kmean = jnp.mean(kb.astype(jnp.float32), axis=2)
    pos = pos0 + jnp.arange(tq)
    qblk = pos // MOBA_BLOCK
    gate = jnp.einsum('bthd,bnhd->bhtn', q.astype(jnp.float32), kmean)
    nb_g = max(nb, MOBA_TOPK)
    gate = jnp.pad(gate, ((0, 0), (0, 0), (0, 0), (0, nb_g - nb)))
    past = jnp.arange(nb_g)[None, None, None, :] < qblk[None, None, :, None]
    gate = jnp.where(past, gate, -jnp.inf)
    _, top = lax.top_k(gate, MOBA_TOPK)
    top = jnp.minimum(top, nb - 1)
    own = jnp.broadcast_to(qblk[None, None, :, None], (b, h, tq, 1))
    sel = jnp.concatenate([top, own], axis=-1).astype(jnp.int32)
    n_chunks = tq // q_chunk
    q_c = jnp.moveaxis(q.reshape(b, n_chunks, q_chunk, h, dh), 1, 0)
    sel_c = jnp.moveaxis(sel.reshape(b, h, n_chunks, q_chunk, MOBA_TOPK + 1), 2, 0)
    pos_c = pos.reshape(n_chunks, q_chunk)
    bi = jnp.arange(b)[:, None, None, None]
    hi = jnp.arange(h)[None, :, None, None]
    slot = jnp.arange(MOBA_TOPK + 1)[:, None]
    rel_t = rel_bias.T

    def chunk(args):
        qc, sc, pc = args
        kg = kb[bi, sc, :, hi]
        vg = vb[bi, sc, :, hi]
        logits = jnp.einsum('bchd,bhcsld->bhcsl', qc, kg).astype(jnp.float32) * (dh ** -0.5)
        kpos = sc[..., None] * MOBA_BLOCK + jnp.arange(MOBA_BLOCK)
        dist = pc[None, None, :, None, None] - kpos
        cblk = (pc // MOBA_BLOCK)[None, None, :, None, None]
        valid = jnp.where(slot == MOBA_TOPK, dist >= 0, slot < cblk)
        bias = rel_t[hi[..., None], t5_bucket(jnp.maximum(dist, 0))]
        logits = jnp.where(valid, logits + bias, -jnp.inf)
        p = jax.nn.softmax(logits, axis=(-2, -1))
        return jnp.einsum('bhcsl,bhcsld->bchd', p.astype(vg.dtype), vg)

    out = lax.map(chunk, (q_c, sel_c, pos_c))
    return jnp.moveaxis(out, 0, 1).reshape(b, tq, h * dh)


def layer_a(x, mk, mv, conv_buf, h0, pos0, g_mix, g_mlp, w_in, w_out, conv_w, conv_b,
            ga_w, ga_b, gx_w, gx_b, lam, w_up, w_down):
    hn = rmsnorm(x, g_mix)
    u, gate, qx = jnp.split(hn @ w_in, [W_A, 2 * W_A], axis=-1)
    y, new_buf, h_last = rglru(u, conv_buf, h0, pos0, conv_w, conv_b, ga_w, ga_b, gx_w, gx_b, lam)
    y = jax.nn.gelu(gate) * y
    x = x + jnp.concatenate([y, cross_attend(qx, mk, mv)], axis=-1) @ w_out
    x = x + sq_relu_mlp(rmsnorm(x, g_mlp), w_up, w_down)
    return x, new_buf, h_last


def layer_b(x, mk, mv, k_past, v_past, pos0, q_chunk, g_mix, g_mlp, w_in, w_out, rel_bias, w_up, w_down):
    b, t, _ = x.shape
    hn = rmsnorm(x, g_mix)
    q, k, v, qx = jnp.split(hn @ w_in, [W_B, 2 * W_B, 3 * W_B], axis=-1)
    q = q.reshape(b, t, B_HEADS, B_HEAD_DIM)
    k = k.reshape(b, t, B_HEADS, B_HEAD_DIM)
    v = v.reshape(b, t, B_HEADS, B_HEAD_DIM)
    pad = (-(pos0 + t)) % MOBA_BLOCK
    zeros = jnp.zeros((b, pad, B_HEADS, B_HEAD_DIM), x.dtype)
    k_all = jnp.concatenate([k_past, k, zeros], axis=1)
    v_all = jnp.concatenate([v_past, v, zeros], axis=1)
    y = moba_attention(q, k_all, v_all, pos0, rel_bias, q_chunk)
    x = x + jnp.concatenate([y, cross_attend(qx, mk, mv)], axis=-1) @ w_out
    x = x + sq_relu_mlp(rmsnorm(x, g_mlp), w_up, w_down)
    return x, k, v


def setup_inputs(seed: int = 0) -> dict:
    key = jax.random.key(seed)
    ks = jax.random.split(key, 32)
    f32 = jnp.float32
    n_pages = PAST_LEN // PAGE_SIZE
    n_used = DEC_BATCH * n_pages
    n_phys = n_used + (n_used + 3) // 4

    def nrm(k, shape, scale):
        return jax.random.normal(k, shape, f32) * scale

    a0 = jax.random.uniform(ks[22], (N_A_LAYERS, W_A), f32, 0.9, 0.999)
    return {
        'x_prompt': nrm(ks[0], (BATCH, SEQ, D_MODEL), 1.0),
        'x_sample': nrm(ks[1], (DEC_BATCH, DEC_SEQ, D_MODEL), 1.0),
        'cache_k': nrm(ks[2], (N_B_LAYERS, n_phys, PAGE_SIZE, B_HEADS, B_HEAD_DIM), 1.0),
        'cache_v': nrm(ks[3], (N_B_LAYERS, n_phys, PAGE_SIZE, B_HEADS, B_HEAD_DIM), 1.0),
        'state_conv': nrm(ks[4], (N_A_LAYERS, DEC_BATCH, CONV_WIDTH - 1, W_A), 1.0),
        'state_h': nrm(ks[5], (N_A_LAYERS, DEC_BATCH, W_A), 1.0),
        'cache_mem_k': nrm(ks[6], (DEPTH, DEC_BATCH, MEM_LEN, X_HEADS, X_HEAD_DIM), 1.0),
        'cache_mem_v': nrm(ks[7], (DEPTH, DEC_BATCH, MEM_LEN, X_HEADS, X_HEAD_DIM), 1.0),
        'page_table': jax.random.permutation(ks[8], n_phys)[:n_used].reshape(DEC_BATCH, n_pages).astype(jnp.int32),
        'mem_prompt': nrm(ks[9], (BATCH, MEM_LEN, D_MODEL), 1.0),
        'norm_mix': 1.0 + nrm(ks[10], (DEPTH, D_MODEL), 0.1),
        'norm_mlp': 1.0 + nrm(ks[11], (DEPTH, D_MODEL), 0.1),
        'norm_mem': 1.0 + nrm(ks[12], (DEPTH, D_MODEL), 0.1),
        'norm_final': 1.0 + nrm(ks[13], (D_MODEL,), 0.1),
        'w_in_a': nrm(ks[14], (N_A_LAYERS, D_MODEL, 2 * W_A + W_X), D_MODEL ** -0.5),
        'w_out_a': nrm(ks[15], (N_A_LAYERS, W_A + W_X, D_MODEL), (W_A + W_X) ** -0.5),
        'conv_w': nrm(ks[16], (N_A_LAYERS, CONV_WIDTH, W_A), CONV_WIDTH ** -0.5),
        'conv_b': nrm(ks[17], (N_A_LAYERS, W_A), 0.02),
        'gate_a_w': nrm(ks[18], (N_A_LAYERS, LRU_BLOCKS, LRU_BLOCK_DIM, LRU_BLOCK_DIM), LRU_BLOCK_DIM ** -0.5),
        'gate_a_b': nrm(ks[19], (N_A_LAYERS, W_A), 0.02),
        'gate_x_w': nrm(ks[20], (N_A_LAYERS, LRU_BLOCKS, LRU_BLOCK_DIM, LRU_BLOCK_DIM), LRU_BLOCK_DIM ** -0.5),
        'gate_x_b': nrm(ks[21], (N_A_LAYERS, W_A), 0.02),
        'lru_lambda': jnp.log(a0) - jnp.log1p(-a0),
        'w_in_b': nrm(ks[23], (N_B_LAYERS, D_MODEL, 3 * W_B + W_X), D_MODEL ** -0.5),
        'w_out_b': nrm(ks[24], (N_B_LAYERS, W_B + W_X, D_MODEL), (W_B + W_X) ** -0.5),
        'rel_bias': nrm(ks[25], (REL_BUCKETS, B_HEADS), 0.5),
        'w_mem_kv': nrm(ks[26], (DEPTH, D_MODEL, 2 * W_X), D_MODEL ** -0.5),
        'w_up': nrm(ks[27], (DEPTH, D_MODEL, D_FF), D_MODEL ** -0.5),
        'w_down': nrm(ks[28], (DEPTH, D_FF, D_MODEL), D_FF ** -0.5),
    }


def reference(x_prompt, x_sample, cache_k, cache_v, state_conv, state_h, cache_mem_k, cache_mem_v,
              page_table, mem_prompt, norm_mix, norm_mlp, norm_mem, norm_final, w_in_a, w_out_a,
              conv_w, conv_b, gate_a_w, gate_a_b, gate_x_w, gate_x_b, lru_lambda, w_in_b, w_out_b,
              rel_bias, w_mem_kv, w_up, w_down):
    n_pages = PAST_LEN // PAGE_SIZE
    hp, hs = x_prompt, x_sample
    kp_l, vp_l, ks_l, vs_l = [], [], [], []
    cp_l, hp_l, cs_l, hs_l = [], [], [], []
    mkp_l, mvp_l = [], []
    for i in range(DEPTH):
        mk_p, mv_p = mem_kv(mem_prompt, norm_mem[i], w_mem_kv[i])
        mkp_l.append(mk_p)
        mvp_l.append(mv_p)
        mk_s, mv_s = cache_mem_k[i], cache_mem_v[i]
        j = i // N_MIXERS
        if i % N_MIXERS == 0:
            wa = (norm_mix[i], norm_mlp[i], w_in_a[j], w_out_a[j], conv_w[j], conv_b[j],
                  gate_a_w[j], gate_a_b[j], gate_x_w[j], gate_x_b[j], lru_lambda[j], w_up[i], w_down[i])
            buf0 = jnp.zeros((BATCH, CONV_WIDTH - 1, W_A), x_prompt.dtype)
            h00 = jnp.zeros((BATCH, W_A), x_prompt.dtype)
            hp, cbuf_p, hl_p = layer_a(hp, mk_p, mv_p, buf0, h00, 0, *wa)
            hs, cbuf_s, hl_s = layer_a(hs, mk_s, mv_s, state_conv[j], state_h[j], PAST_LEN, *wa)
            cp_l.append(cbuf_p)
            hp_l.append(hl_p)
            cs_l.append(cbuf_s)
            hs_l.append(hl_s)
        else:
            wb = (norm_mix[i], norm_mlp[i], w_in_b[j], w_out_b[j], rel_bias, w_up[i], w_down[i])
            empty = jnp.zeros((BATCH, 0, B_HEADS, B_HEAD_DIM), x_prompt.dtype)
            hp, k_new_p, v_new_p = layer_b(hp, mk_p, mv_p, empty, empty, 0, MOBA_Q_CHUNK, *wb)
            k_past = cache_k[j, page_table].reshape(DEC_BATCH, n_pages * PAGE_SIZE, B_HEADS, B_HEAD_DIM)
            v_past = cache_v[j, page_table].reshape(DEC_BATCH, n_pages * PAGE_SIZE, B_HEADS, B_HEAD_DIM)
            hs, k_new_s, v_new_s = layer_b(hs, mk_s, mv_s, k_past, v_past, PAST_LEN, 1, *wb)
            kp_l.append(k_new_p)
            vp_l.append(v_new_p)
            ks_l.append(k_new_s)
            vs_l.append(v_new_s)
    y_prompt = rmsnorm(hp, norm_final)
    y_sample = rmsnorm(hs, norm_final)
    return (y_prompt, y_sample, jnp.stack(kp_l), jnp.stack(vp_l), jnp.stack(ks_l), jnp.stack(vs_l),
            jnp.stack(cp_l), jnp.stack(hp_l), jnp.stack(cs_l), jnp.stack(hs_l),
            jnp.stack(mkp_l), jnp.stack(mvp_l))
```

```python
import functools
import math

import jax
import jax.numpy as jnp
from jax import lax
from jax.experimental import pallas as pl
from jax.experimental.pallas import tpu as pltpu

F32 = jnp.float32
BF16 = jnp.bfloat16

D_MODEL = 1024
BATCH = 8
SEQ = 2048
DEPTH = 4
DEC_BATCH = 128
DEC_SEQ = 8
PAST_LEN = 2048
PAGE_SIZE = 128
N_PAGES = PAST_LEN // PAGE_SIZE
W_A = D_MODEL
LRU_BLOCKS = 8
LRU_BLOCK_DIM = W_A // LRU_BLOCKS
CONV_WIDTH = 4
LRU_C = 8.0
B_HEADS = 8
B_HEAD_DIM = D_MODEL // B_HEADS
W_B = B_HEADS * B_HEAD_DIM
MOBA_BLOCK = 256
MOBA_TOPK = 3
N_BLOCKS = SEQ // MOBA_BLOCK
REL_BUCKETS = 32
REL_MAX_DIST = 128
X_HEADS = 4
X_HEAD_DIM = 128
W_X = X_HEADS * X_HEAD_DIM
MEM_LEN = 256
D_FF = 4 * D_MODEL
EPS = 1e-6

SUBLANES = 8
VMEM_LIMIT = 56 * 1024 * 1024
NEG_INF = float("-inf")


def _cparams(*sem):
    return pltpu.CompilerParams(dimension_semantics=sem, vmem_limit_bytes=VMEM_LIMIT)


def _dot(a, b):
    return jnp.dot(a, b, preferred_element_type=F32)


def _dot_nt(a, b, precision=None):
    return lax.dot_general(a, b, (((1,), (1,)), ((), ())), preferred_element_type=F32, precision=precision)


def _rms(x, g):
    return x * lax.rsqrt(jnp.mean(x * x, axis=-1, keepdims=True) + EPS) * g


def _t5_thresholds():
    max_exact = REL_BUCKETS // 2
    thr = []
    for k in range(1, REL_BUCKETS - max_exact):
        d = max_exact
        while int(math.log(d / max_exact) / math.log(REL_MAX_DIST / max_exact) * (REL_BUCKETS - max_exact)) < k:
            d += 1
        thr.append(d)
    return thr


T5_THRESHOLDS = _t5_thresholds()


def _mem_kv_kernel(x_ref, g_ref, w_ref, k_ref, v_ref):
    hn = _rms(x_ref[...], g_ref[...]).astype(BF16)
    r = _dot(hn, w_ref[...])
    k_ref[...] = r[:, :W_X]
    v_ref[...] = r[:, W_X:]


def _mem_kv(mem2d, norm_mem, w_bf):
    m = mem2d.shape[0]
    tm = 512
    return pl.pallas_call(
        _mem_kv_kernel,
        grid=(DEPTH, m // tm),
        in_specs=[
            pl.BlockSpec((tm, D_MODEL), lambda l, i: (i, 0)),
            pl.BlockSpec((None, 1, D_MODEL), lambda l, i: (l, 0, 0)),
            pl.BlockSpec((None, D_MODEL, 2 * W_X), lambda l, i: (l, 0, 0)),
        ],
        out_specs=[pl.BlockSpec((None, tm, W_X), lambda l, i: (l, i, 0))] * 2,
        out_shape=[jax.ShapeDtypeStruct((DEPTH, m, W_X), F32)] * 2,
        compiler_params=_cparams("parallel", "parallel"),
        name="mem_kv",
    )(mem2d, norm_mem, w_bf)


def _norm_proj_kernel(x_ref, g_ref, w_ref, *out_refs, widths):
    hn = _rms(x_ref[...], g_ref[...]).astype(BF16)
    off = 0
    for o_ref, n in zip(out_refs, widths):
        o_ref[...] = _dot(hn, w_ref[:, off:off + n]).astype(o_ref.dtype)
        off += n


def _norm_proj(x, g, w_bf, widths, dtypes):
    m = x.shape[0]
    tm = 512
    n_all = w_bf.shape[1]
    return pl.pallas_call(
        functools.partial(_norm_proj_kernel, widths=widths),
        grid=(m // tm,),
        in_specs=[
            pl.BlockSpec((tm, D_MODEL), lambda i: (i, 0)),
            pl.BlockSpec((1, D_MODEL), lambda i: (0, 0)),
            pl.BlockSpec((D_MODEL, n_all), lambda i: (0, 0)),
        ],
        out_specs=[pl.BlockSpec((tm, n), lambda i: (i, 0)) for n in widths],
        out_shape=[jax.ShapeDtypeStruct((m, n), dt) for n, dt in zip(widths, dtypes)],
        compiler_params=_cparams("parallel"),
        name="norm_proj",
    )(x, g, w_bf)


def _softmax_pv(s, v_bf):
    m = jnp.max(s, axis=-1, keepdims=True)
    p = jnp.exp(s - m)
    l = jnp.sum(p, axis=-1, keepdims=True)
    return _dot(p.astype(BF16), v_bf) / l


def _xattn_prompt_kernel(q_ref, k_ref, v_ref, o_ref):
    scale = X_HEAD_DIM ** -0.5
    for h in range(X_HEADS):
        sl = slice(h * X_HEAD_DIM, (h + 1) * X_HEAD_DIM)
        s = _dot_nt(q_ref[:, sl], k_ref[:, sl].astype(BF16)) * scale
        o_ref[:, sl] = _softmax_pv(s, v_ref[:, sl].astype(BF16)).astype(o_ref.dtype)


def _xattn_prompt(qx, memk, memv, layer):
    tq = 512
    nt = SEQ // tq
    kv_spec = pl.BlockSpec((None, MEM_LEN, W_X), lambda b, t: (layer, b, 0))
    return pl.pallas_call(
        _xattn_prompt_kernel,
        grid=(BATCH, nt),
        in_specs=[pl.BlockSpec((tq, W_X), lambda b, t: (b * nt + t, 0)), kv_spec, kv_spec],
        out_specs=pl.BlockSpec((tq, W_X), lambda b, t: (b * nt + t, 0)),
        out_shape=jax.ShapeDtypeStruct((BATCH * SEQ, W_X), BF16),
        compiler_params=_cparams("parallel", "parallel"),
        name="xattn_prompt",
    )(qx, memk, memv)


def _head_block_mask(rows, cols, rows_per_head, cols_per_head):
    r = lax.broadcasted_iota(jnp.int32, (rows, cols), 0) // rows_per_head
    c = lax.broadcasted_iota(jnp.int32, (rows, cols), 1) // cols_per_head
    return r == c


def _fold_heads(o, mask, n_heads):
    o = jnp.where(mask, o, 0.0)
    acc = o[0:DEC_SEQ]
    for h in range(1, n_heads):
        acc = acc + o[h * DEC_SEQ:(h + 1) * DEC_SEQ]
    return acc


def _xattn_sample_kernel(q_ref, k_ref, v_ref, o_ref, *, group):
    scale = X_HEAD_DIM ** -0.5
    mask = _head_block_mask(X_HEADS * DEC_SEQ, W_X, DEC_SEQ, X_HEAD_DIM)
    for g in range(group):
        rows = slice(g * DEC_SEQ, (g + 1) * DEC_SEQ)
        q = q_ref[rows, :]
        qbd = jnp.where(mask, jnp.concatenate([q] * X_HEADS, axis=0), 0.0).astype(BF16)
        s = _dot_nt(qbd, k_ref[g].astype(BF16)) * scale
        o = _softmax_pv(s, v_ref[g].astype(BF16))
        o_ref[rows, :] = _fold_heads(o, mask, X_HEADS)


def _xattn_sample(qx, cmk, cmv, layer):
    group = 8
    kv_spec = pl.BlockSpec((None, group, MEM_LEN, W_X), lambda i: (layer, i, 0, 0))
    return pl.pallas_call(
        functools.partial(_xattn_sample_kernel, group=group),
        grid=(DEC_BATCH // group,),
        in_specs=[pl.BlockSpec((group * DEC_SEQ, W_X), lambda i: (i, 0)), kv_spec, kv_spec],
        out_specs=pl.BlockSpec((group * DEC_SEQ, W_X), lambda i: (i, 0)),
        out_shape=jax.ShapeDtypeStruct((DEC_BATCH * DEC_SEQ, W_X), F32),
        compiler_params=_cparams("parallel"),
        name="xattn_sample",
    )(qx, cmk, cmv)


def _post_kernel(x_ref, y_ref, c_ref, wy_ref, wc_ref, g_ref, wu_ref, wd_ref, gf_ref, o_ref,
                 xn_scr, hn_scr, acc_scr, *, final_norm):
    f = pl.program_id(1)

    @pl.when(f == 0)
    def _():
        xn = (x_ref[...] + _dot(y_ref[...].astype(BF16), wy_ref[...])
              + _dot(c_ref[...].astype(BF16), wc_ref[...]))
        xn_scr[...] = xn
        hn_scr[...] = _rms(xn, g_ref[...]).astype(BF16)
        acc_scr[...] = jnp.zeros_like(acc_scr)

    h = _dot(hn_scr[...], wu_ref[...])
    h = jnp.square(jnp.maximum(h, 0.0)).astype(BF16)
    acc_scr[...] += _dot(h, wd_ref[...])

    @pl.when(f == pl.num_programs(1) - 1)
    def _():
        out = xn_scr[...] + acc_scr[...]
        if final_norm:
            out = _rms(out, gf_ref[...])
        o_ref[...] = out


def _post(x, y, ca, wy, wc, g, wu, wd, gf, final_norm):
    m = x.shape[0]
    tm, tf = 512, 1024
    row = lambda i, f: (i, 0)
    const = lambda i, f: (0, 0)
    return pl.pallas_call(
        functools.partial(_post_kernel, final_norm=final_norm),
        grid=(m // tm, D_FF // tf),
        in_specs=[
            pl.BlockSpec((tm, D_MODEL), row),
            pl.BlockSpec((tm, y.shape[1]), row),
            pl.BlockSpec((tm, W_X), row),
            pl.BlockSpec(wy.shape, const),
            pl.BlockSpec(wc.shape, const),
            pl.BlockSpec((1, D_MODEL), const),
            pl.BlockSpec((D_MODEL, tf), lambda i, f: (0, f)),
            pl.BlockSpec((tf, D_MODEL), lambda i, f: (f, 0)),
            pl.BlockSpec((1, D_MODEL), const),
        ],
        out_specs=pl.BlockSpec((tm, D_MODEL), row),
        out_shape=jax.ShapeDtypeStruct((m, D_MODEL), F32),
        scratch_shapes=[
            pltpu.VMEM((tm, D_MODEL), F32),
            pltpu.VMEM((tm, D_MODEL), BF16),
            pltpu.VMEM((tm, D_MODEL), F32),
        ],
        compiler_params=_cparams("parallel", "arbitrary"),
        name="post",
    )(x, y, ca, wy, wc, g, wu, wd, gf)


def _gelu_tanh(x):
    return x * (0.5 * (1.0 + jnp.tanh(math.sqrt(2.0 / math.pi) * (x + 0.044715 * (x * x * x)))))


def _conv_rows(u, u_prev, cw_ref, cb_ref):
    rows = u.shape[0]
    g = rows // SUBLANES
    u3 = u.reshape(g, SUBLANES, W_A)
    p3 = u_prev.reshape(g, SUBLANES, W_A)
    sub = lax.broadcasted_iota(jnp.int32, (g, SUBLANES, W_A), 1)
    xc = cb_ref[...] + cw_ref[CONV_WIDTH - 1:CONV_WIDTH, :] * u3
    for j in range(1, CONV_WIDTH):
        sh = jnp.where(sub >= j, pltpu.roll(u3, j, axis=1), pltpu.roll(p3, j, axis=1))
        xc = xc + cw_ref[CONV_WIDTH - 1 - j:CONV_WIDTH - j, :] * sh
    return xc.reshape(rows, W_A)


def _lru_coeffs(xc, wcat_ref, gab_ref, gxb_ref, lam_ref, first_row, a_scr, b_scr):
    z = -lam_ref[...]
    softplus = jnp.maximum(z, 0.0) + jnp.log1p(jnp.exp(-jnp.abs(z)))
    for n in range(LRU_BLOCKS):
        sl = slice(n * LRU_BLOCK_DIM, (n + 1) * LRU_BLOCK_DIM)
        xn = xc[:, sl]
        ra = _dot(xn.astype(BF16), wcat_ref[n])
        r = jax.nn.sigmoid(ra[:, :LRU_BLOCK_DIM] + gab_ref[:, sl])
        ig = jax.nn.sigmoid(ra[:, LRU_BLOCK_DIM:] + gxb_ref[:, sl])
        log_a = -LRU_C * r * softplus[:, sl]
        th = jnp.tanh(log_a)
        mult = jnp.sqrt(-2.0 * th / (1.0 - th))
        if first_row is not None:
            mult = jnp.where(first_row, 1.0, mult)
        a_scr[:, sl] = jnp.exp(log_a)
        b_scr[:, sl] = mult * ig * xn


def _scan_within_tiles(a, b):
    rows = a.shape[0]
    g = rows // SUBLANES
    a3 = a.reshape(g, SUBLANES, W_A)
    b3 = b.reshape(g, SUBLANES, W_A)
    sub = lax.broadcasted_iota(jnp.int32, (g, SUBLANES, W_A), 1)
    for s in (1, 2, 4):
        keep = sub >= s
        b3 = jnp.where(keep, a3 * pltpu.roll(b3, s, axis=1) + b3, b3)
        a3 = jnp.where(keep, a3 * pltpu.roll(a3, s, axis=1), a3)
    return a3.reshape(rows, W_A), b3.reshape(rows, W_A)


def _lru_prompt_kernel(u_ref, gate_ref, wcat_ref, cw_ref, cb_ref, gab_ref, gxb_ref, lam_ref,
                       y_ref, utail_ref, htail_ref, tail_scr, carry_scr, a_scr, b_scr, h_scr, *, tile):
    t = pl.program_id(1)

    @pl.when(t == 0)
    def _():
        tail_scr[...] = jnp.zeros_like(tail_scr)
        carry_scr[...] = jnp.zeros_like(carry_scr)

    u = u_ref[...]
    u_prev = jnp.concatenate([tail_scr[...], u[:tile - SUBLANES]], axis=0)
    tail_scr[...] = u[tile - SUBLANES:]
    xc = _conv_rows(u, u_prev, cw_ref, cb_ref)
    row = lax.broadcasted_iota(jnp.int32, (tile, LRU_BLOCK_DIM), 0)
    first_row = jnp.logical_and(row == 0, t == 0)
    _lru_coeffs(xc, wcat_ref, gab_ref, gxb_ref, lam_ref, first_row, a_scr, b_scr)
    a, b = _scan_within_tiles(a_scr[...], b_scr[...])
    a_scr[...] = a
    b_scr[...] = b

    def step(g, carry):
        rows = pl.ds(pl.multiple_of(g * SUBLANES, SUBLANES), SUBLANES)
        h = b_scr[rows, :] + a_scr[rows, :] * carry
        h_scr[rows, :] = h
        return jnp.broadcast_to(h[SUBLANES - 1:SUBLANES, :], (SUBLANES, W_A))

    carry_scr[...] = lax.fori_loop(0, tile // SUBLANES, step, carry_scr[...])
    y_ref[...] = (_gelu_tanh(gate_ref[...]) * h_scr[...]).astype(y_ref.dtype)
    utail_ref[...] = u[tile - SUBLANES:]
    htail_ref[...] = h_scr[tile - SUBLANES:, :]


def _lru_prompt(u, gate, wcat, cw, cb, gab, gxb, lam):
    tile = 256
    nt = SEQ // tile
    row = lambda b, t: (b * nt + t, 0)
    const2 = lambda b, t: (0, 0)
    vec = pl.BlockSpec((1, W_A), const2)
    tail_spec = pl.BlockSpec((None, SUBLANES, W_A), lambda b, t: (b, 0, 0))
    return pl.pallas_call(
        functools.partial(_lru_prompt_kernel, tile=tile),
        grid=(BATCH, nt),
        in_specs=[
            pl.BlockSpec((tile, W_A), row),
            pl.BlockSpec((tile, W_A), row),
            pl.BlockSpec(wcat.shape, lambda b, t: (0, 0, 0)),
            pl.BlockSpec((CONV_WIDTH, W_A), const2),
            vec, vec, vec, vec,
        ],
        out_specs=[pl.BlockSpec((tile, W_A), row), tail_spec, tail_spec],
        out_shape=[
            jax.ShapeDtypeStruct((BATCH * SEQ, W_A), BF16),
            jax.ShapeDtypeStruct((BATCH, SUBLANES, W_A), F32),
            jax.ShapeDtypeStruct((BATCH, SUBLANES, W_A), F32),
        ],
        scratch_shapes=[
            pltpu.VMEM((SUBLANES, W_A), F32),
            pltpu.VMEM((SUBLANES, W_A), F32),
            pltpu.VMEM((tile, W_A), F32),
            pltpu.VMEM((tile, W_A), F32),
            pltpu.VMEM((tile, W_A), F32),
        ],
        compiler_params=_cparams("parallel", "arbitrary"),
        name="lru_prompt",
    )(u, gate, wcat, cw, cb, gab, gxb, lam)


def _lru_sample_kernel(u_ref, gate_ref, prev_ref, h0_ref, wcat_ref, cw_ref, cb_ref, gab_ref, gxb_ref, lam_ref,
                       y_ref, h_ref, a_scr, b_scr):
    xc = _conv_rows(u_ref[...], prev_ref[...], cw_ref, cb_ref)
    _lru_coeffs(xc, wcat_ref, gab_ref, gxb_ref, lam_ref, None, a_scr, b_scr)
    a = a_scr[...]
    _, h = _scan_within_tiles(a, b_scr[...] + a * h0_ref[...])
    h_ref[...] = h
    y_ref[...] = (_gelu_tanh(gate_ref[...]) * h).astype(y_ref.dtype)


def _lru_sample(u, gate, prev, h0pad, wcat, cw, cb, gab, gxb, lam):
    m = u.shape[0]
    tile = 256
    row = lambda i: (i, 0)
    const2 = lambda i: (0, 0)
    vec = pl.BlockSpec((1, W_A), const2)
    blk = pl.BlockSpec((tile, W_A), row)
    return pl.pallas_call(
        _lru_sample_kernel,
        grid=(m // tile,),
        in_specs=[blk, blk, blk, blk,
                  pl.BlockSpec(wcat.shape, lambda i: (0, 0, 0)),
                  pl.BlockSpec((CONV_WIDTH, W_A), const2),
                  vec, vec, vec, vec],
        out_specs=[blk, blk],
        out_shape=[jax.ShapeDtypeStruct((m, W_A), BF16), jax.ShapeDtypeStruct((m, W_A), F32)],
        scratch_shapes=[pltpu.VMEM((tile, W_A), F32), pltpu.VMEM((tile, W_A), F32)],
        compiler_params=_cparams("parallel"),
        name="lru_sample",
    )(u, gate, prev, h0pad, wcat, cw, cb, gab, gxb, lam)


def _bias_tile_kernel(rel_ref, o_ref):
    h = pl.program_id(0)
    w = pl.program_id(1)
    shape = (MOBA_BLOCK, MOBA_BLOCK)
    r = lax.broadcasted_iota(jnp.int32, shape, 0)
    c = lax.broadcasted_iota(jnp.int32, shape, 1)
    dist = jnp.maximum(w * MOBA_BLOCK + r - c, 0)
    large = jnp.full(shape, REL_BUCKETS // 2, jnp.int32)
    for thr in T5_THRESHOLDS:
        large = large + (dist >= thr).astype(jnp.int32)
    bucket = jnp.where(dist < REL_BUCKETS // 2, dist, large)
    out = jnp.zeros(shape, F32)
    for b in range(REL_BUCKETS):
        out = jnp.where(bucket == b, rel_ref[b, h], out)
    o_ref[...] = out


def _bias_tiles(rel_bias):
    return pl.pallas_call(
        _bias_tile_kernel,
        grid=(B_HEADS, 2),
        in_specs=[pl.BlockSpec(memory_space=pltpu.SMEM)],
        out_specs=pl.BlockSpec((None, None, MOBA_BLOCK, MOBA_BLOCK), lambda h, w: (h, w, 0, 0)),
        out_shape=jax.ShapeDtypeStruct((B_HEADS, 2, MOBA_BLOCK, MOBA_BLOCK), F32),
        compiler_params=_cparams("parallel", "parallel"),
        name="bias_tiles",
    )(rel_bias)


def _select_topk(gate, n_past):
    blk = lax.broadcasted_iota(jnp.int32, gate.shape, 1)
    past = blk < n_past
    g = jnp.where(past, gate, NEG_INF)
    rank = jnp.zeros(gate.shape, jnp.int32)
    for m in range(gate.shape[1]):
        gm = g[:, m:m + 1]
        beats = jnp.logical_or(gm > g, jnp.logical_and(gm == g, blk > m))
        rank = rank + beats.astype(jnp.int32)
    return jnp.where(jnp.logical_and(past, rank < MOBA_TOPK), 1.0, 0.0)


def _moba_prompt_kernel(rel_ref, q_ref, k_ref, v_ref, bias_ref, o_ref, kb_scr, vb_scr, kmean_scr):
    h = pl.program_id(1)
    i = pl.program_id(2)
    scale = B_HEAD_DIM ** -0.5

    @pl.when(i == 0)
    def _():
        for n in range(N_BLOCKS):
            rows = slice(n * MOBA_BLOCK, (n + 1) * MOBA_BLOCK)
            kn = k_ref[rows, :]
            kmean_scr[n:n + 1, :] = jnp.mean(kn, axis=0, keepdims=True)
            kb_scr[rows, :] = kn.astype(BF16)
            vb_scr[rows, :] = v_ref[rows, :].astype(BF16)

    q = q_ref[...]
    qb = q.astype(BF16)
    gate = _dot_nt(q, kmean_scr[...], precision=lax.Precision.HIGHEST)
    sel = _select_topk(gate, i)
    blk = lax.broadcasted_iota(jnp.int32, sel.shape, 1)
    far_bias = rel_ref[REL_BUCKETS - 1, h]

    own = pl.ds(pl.multiple_of(i * MOBA_BLOCK, MOBA_BLOCK), MOBA_BLOCK)
    r = lax.broadcasted_iota(jnp.int32, (MOBA_BLOCK, MOBA_BLOCK), 0)
    c = lax.broadcasted_iota(jnp.int32, (MOBA_BLOCK, MOBA_BLOCK), 1)
    s = _dot_nt(qb, kb_scr[own, :]) * scale + bias_ref[0]
    s = jnp.where(c <= r, s, NEG_INF)
    m0 = jnp.max(s, axis=-1, keepdims=True)
    p = jnp.exp(s - m0)
    l0 = jnp.sum(p, axis=-1, keepdims=True)
    acc0 = _dot(p.astype(BF16), vb_scr[own, :])

    def body(n, carry):
        m, l, acc = carry
        rows = pl.ds(pl.multiple_of(n * MOBA_BLOCK, MOBA_BLOCK), MOBA_BLOCK)
        chosen = jnp.max(jnp.where(blk == n, sel, 0.0), axis=-1, keepdims=True)
        bias = jnp.where(n == i - 1, bias_ref[1], far_bias)
        s = _dot_nt(qb, kb_scr[rows, :]) * scale + bias
        s = jnp.where(chosen > 0.0, s, NEG_INF)
        m_new = jnp.maximum(m, jnp.max(s, axis=-1, keepdims=True))
        alpha = jnp.exp(m - m_new)
        p = jnp.exp(s - m_new)
        l = alpha * l + jnp.sum(p, axis=-1, keepdims=True)
        acc = alpha * acc + _dot(p.astype(BF16), vb_scr[rows, :])
        return m_new, l, acc

    _, l, acc = lax.fori_loop(0, i, body, (m0, l0, acc0))
    o_ref[...] = (acc / l).astype(o_ref.dtype)


def _moba_prompt(q, k, v, bias_tiles, rel_bias):
    kv_spec = pl.BlockSpec((SEQ, B_HEAD_DIM), lambda b, h, i: (b, h))
    q_spec = pl.BlockSpec((MOBA_BLOCK, B_HEAD_DIM), lambda b, h, i: (b * N_BLOCKS + i, h))
    return pl.pallas_call(
        _moba_prompt_kernel,
        grid=(BATCH, B_HEADS, N_BLOCKS),
        in_specs=[
            pl.BlockSpec(memory_space=pltpu.SMEM),
            q_spec, kv_spec, kv_spec,
            pl.BlockSpec((None, 2, MOBA_BLOCK, MOBA_BLOCK), lambda b, h, i: (h, 0, 0, 0)),
        ],
        out_specs=q_spec,
        out_shape=jax.ShapeDtypeStruct((BATCH * SEQ, W_B), BF16),
        scratch_shapes=[
            pltpu.VMEM((SEQ, B_HEAD_DIM), BF16),
            pltpu.VMEM((SEQ, B_HEAD_DIM), BF16),
            pltpu.VMEM((N_BLOCKS, B_HEAD_DIM), F32),
        ],
        compiler_params=_cparams("parallel", "parallel", "arbitrary"),
        name="moba_prompt",
    )(rel_bias, q, k, v, bias_tiles)


def _moba_sample_kernel(pt_ref, q_ref, kn_ref, vn_ref, bprev_ref, bnew_ref, far_ref, *refs):
    del pt_ref
    k_pages = refs[:N_PAGES]
    v_pages = refs[N_PAGES:2 * N_PAGES]
    o_ref = refs[2 * N_PAGES]
    scale = B_HEAD_DIM ** -0.5
    rows = B_HEADS * DEC_SEQ
    pages_per_block = MOBA_BLOCK // PAGE_SIZE
    n_past = PAST_LEN // MOBA_BLOCK

    mask = _head_block_mask(rows, W_B, DEC_SEQ, B_HEAD_DIM)
    qbd = jnp.where(mask, jnp.concatenate([q_ref[...]] * B_HEADS, axis=0), 0.0)
    qb = qbd.astype(BF16)

    kmean = jnp.concatenate(
        [sum(jnp.sum(k_pages[n * pages_per_block + j][...], axis=0, keepdims=True)
             for j in range(pages_per_block)) * (1.0 / MOBA_BLOCK) for n in range(n_past)], axis=0)
    gate = _dot_nt(qbd, kmean, precision=lax.Precision.HIGHEST)
    sel = _select_topk(gate, n_past)

    far_bias = far_ref[...]
    s_tiles = []
    for p in range(N_PAGES):
        n = p // pages_per_block
        s = _dot_nt(qb, k_pages[p][...].astype(BF16)) * scale
        if n == n_past - 1:
            j = p - n * pages_per_block
            s = s + bprev_ref[:, j * PAGE_SIZE:(j + 1) * PAGE_SIZE]
        else:
            s = s + far_bias
        s_tiles.append(jnp.where(sel[:, n:n + 1] > 0.0, s, NEG_INF))
    s_new = _dot_nt(qbd, kn_ref[...]) * scale + bnew_ref[...]
    t_q = lax.broadcasted_iota(jnp.int32, (rows, DEC_SEQ), 0) % DEC_SEQ
    t_k = lax.broadcasted_iota(jnp.int32, (rows, DEC_SEQ), 1)
    s_new = jnp.where(t_k <= t_q, s_new, NEG_INF)

    m_tile = s_tiles[0]
    for s in s_tiles[1:]:
        m_tile = jnp.maximum(m_tile, s)
    m = jnp.maximum(jnp.max(m_tile, axis=-1, keepdims=True), jnp.max(s_new, axis=-1, keepdims=True))
    p_new = jnp.exp(s_new - m)
    l = jnp.sum(p_new, axis=-1, keepdims=True)
    acc = _dot(p_new, vn_ref[...])
    for p in range(N_PAGES):
        pr = jnp.exp(s_tiles[p] - m)
        l = l + jnp.sum(pr, axis=-1, keepdims=True)
        acc = acc + _dot(pr.astype(BF16), v_pages[p][...].astype(BF16))
    o_ref[...] = _fold_heads(acc / l, mask, B_HEADS)


def _moba_sample(q, k_new, v_new, cache_k4, cache_v4, layer, page_table_flat, bias_prev, bias_new, far_col):
    rows = B_HEADS * DEC_SEQ
    tok_spec = pl.BlockSpec((DEC_SEQ, W_B), lambda b, pt: (b, 0))

    def page_spec(p):
        return pl.BlockSpec((None, None, PAGE_SIZE, W_B), lambda b, pt: (layer, pt[b * N_PAGES + p], 0, 0))

    pages = [page_spec(p) for p in range(N_PAGES)]
    grid_spec = pltpu.PrefetchScalarGridSpec(
        num_scalar_prefetch=1,
        grid=(DEC_BATCH,),
        in_specs=[
            tok_spec, tok_spec, tok_spec,
            pl.BlockSpec((rows, MOBA_BLOCK), lambda b, pt: (0, 0)),
            pl.BlockSpec((rows, DEC_SEQ), lambda b, pt: (0, 0)),
            pl.BlockSpec((rows, 1), lambda b, pt: (0, 0)),
        ] + pages + pages,
        out_specs=tok_spec,
    )
    return pl.pallas_call(
        _moba_sample_kernel,
        grid_spec=grid_spec,
        out_shape=jax.ShapeDtypeStruct((DEC_BATCH * DEC_SEQ, W_B), F32),
        compiler_params=_cparams("parallel"),
        name="moba_sample",
    )(page_table_flat, q, k_new, v_new, bias_prev, bias_new, far_col,
      *([cache_k4] * N_PAGES), *([cache_v4] * N_PAGES))


def kernel(x_prompt, x_sample, cache_k, cache_v, state_conv, state_h, cache_mem_k, cache_mem_v, page_table,
           mem_prompt, norm_mix, norm_mlp, norm_mem, norm_final, w_in_a, w_out_a, conv_w, conv_b, gate_a_w,
           gate_a_b, gate_x_w, gate_x_b, lru_lambda, w_in_b, w_out_b, rel_bias, w_mem_kv, w_up, w_down):
    n_prompt = BATCH * SEQ
    n_sample = DEC_BATCH * DEC_SEQ
    xp = x_prompt.reshape(n_prompt, D_MODEL)
    xs = x_sample.reshape(n_sample, D_MODEL)

    w_in_a_bf, w_in_b_bf = w_in_a.astype(BF16), w_in_b.astype(BF16)
    w_out_a_bf, w_out_b_bf = w_out_a.astype(BF16), w_out_b.astype(BF16)
    w_up_bf, w_down_bf = w_up.astype(BF16), w_down.astype(BF16)
    wcat_bf = jnp.concatenate([gate_a_w, gate_x_w], axis=-1).astype(BF16)
    gf = norm_final.reshape(1, D_MODEL)

    memk, memv = _mem_kv(mem_prompt.reshape(BATCH * MEM_LEN, D_MODEL), norm_mem.reshape(DEPTH, 1, D_MODEL),
                         w_mem_kv.astype(BF16))
    cmk = cache_mem_k.reshape(DEPTH, DEC_BATCH, MEM_LEN, W_X)
    cmv = cache_mem_v.reshape(DEPTH, DEC_BATCH, MEM_LEN, W_X)
    cache_k4 = cache_k.reshape(cache_k.shape[0], cache_k.shape[1], PAGE_SIZE, W_B)
    cache_v4 = cache_v.reshape(cache_v.shape[0], cache_v.shape[1], PAGE_SIZE, W_B)
    pt_flat = page_table.reshape(DEC_BATCH * N_PAGES)

    tiles = _bias_tiles(rel_bias)
    rows = B_HEADS * DEC_SEQ
    bias_prev = tiles[:, 1, :DEC_SEQ, :].reshape(rows, MOBA_BLOCK)
    bias_new = tiles[:, 0, :DEC_SEQ, :DEC_SEQ].reshape(rows, DEC_SEQ)
    far_col = jnp.repeat(rel_bias[REL_BUCKETS - 1], DEC_SEQ).reshape(rows, 1)

    kp_l, vp_l, ks_l, vs_l = [], [], [], []
    cp_l, hp_l, cs_l, hs_l = [], [], [], []
    for i in range(DEPTH):
        j = i // 2
        g_mix = norm_mix[i].reshape(1, D_MODEL)
        g_mlp = norm_mlp[i].reshape(1, D_MODEL)
        final = i == DEPTH - 1
        if i % 2 == 0:
            widths = (W_A, W_A, W_X)
            lru_w = (wcat_bf[j], conv_w[j], conv_b[j].reshape(1, W_A), gate_a_b[j].reshape(1, W_A),
                     gate_x_b[j].reshape(1, W_A), lru_lambda[j].reshape(1, W_A))
            w_out = w_out_a_bf[j]
            mix_w = W_A

            u, gate, qx = _norm_proj(xp, g_mix, w_in_a_bf[j], widths, (F32, F32, BF16))
            y_p, utail, htail = _lru_prompt(u, gate, *lru_w)
            cp_l.append(utail[:, SUBLANES - (CONV_WIDTH - 1):, :])
            hp_l.append(htail[:, SUBLANES - 1, :])

            u, gate, qx_s = _norm_proj(xs, g_mix, w_in_a_bf[j], widths, (F32, F32, F32))
            prev = jnp.pad(state_conv[j], ((0, 0), (SUBLANES - (CONV_WIDTH - 1), 0), (0, 0)))
            h0pad = jnp.pad(state_h[j][:, None, :], ((0, 0), (0, DEC_SEQ - 1), (0, 0)))
            y_s, h_s = _lru_sample(u, gate, prev.reshape(n_sample, W_A), h0pad.reshape(n_sample, W_A), *lru_w)
            cs_l.append(u.reshape(DEC_BATCH, DEC_SEQ, W_A)[:, DEC_SEQ - (CONV_WIDTH - 1):, :])
            hs_l.append(h_s.reshape(DEC_BATCH, DEC_SEQ, W_A)[:, DEC_SEQ - 1, :])
        else:
            widths = (W_B, W_B, W_B, W_X)
            w_out = w_out_b_bf[j]
            mix_w = W_B

            q, k, v, qx = _norm_proj(xp, g_mix, w_in_b_bf[j], widths, (F32, F32, F32, BF16))
            y_p = _moba_prompt(q, k, v, tiles, rel_bias)
            kp_l.append(k.reshape(BATCH, SEQ, B_HEADS, B_HEAD_DIM))
            vp_l.append(v.reshape(BATCH, SEQ, B_HEADS, B_HEAD_DIM))

            q, k, v, qx_s = _norm_proj(xs, g_mix, w_in_b_bf[j], widths, (F32, F32, F32, F32))
            y_s = _moba_sample(q, k, v, cache_k4, cache_v4, j, pt_flat, bias_prev, bias_new, far_col)
            ks_l.append(k.reshape(DEC_BATCH, DEC_SEQ, B_HEADS, B_HEAD_DIM))
            vs_l.append(v.reshape(DEC_BATCH, DEC_SEQ, B_HEADS, B_HEAD_DIM))

        ca_p = _xattn_prompt(qx, memk, memv, i)
        ca_s = _xattn_sample(qx_s, cmk, cmv, i)
        post_w = (w_out[:mix_w], w_out[mix_w:], g_mlp, w_up_bf[i], w_down_bf[i], gf, final)
        xp = _post(xp, y_p, ca_p, *post_w)
        xs = _post(xs, y_s, ca_s, *post_w)

    mem_shape = (DEPTH, BATCH, MEM_LEN, X_HEADS, X_HEAD_DIM)
    return (xp.reshape(BATCH, SEQ, D_MODEL), xs.reshape(DEC_BATCH, DEC_SEQ, D_MODEL),
            jnp.stack(kp_l), jnp.stack(vp_l), jnp.stack(ks_l), jnp.stack(vs_l),
            jnp.stack(cp_l), jnp.stack(hp_l), jnp.stack(cs_l), jnp.stack(hs_l),
            memk.reshape(mem_shape), memv.reshape(mem_shape))
```

```python
import functools
import math

import jax
import jax.numpy as jnp
from jax import lax
from jax.experimental import pallas as pl
from jax.experimental.pallas import tpu as pltpu

F32 = jnp.float32
BF16 = jnp.bfloat16

D_MODEL = 1024
BATCH = 8
SEQ = 2048
DEPTH = 4
DEC_BATCH = 128
DEC_SEQ = 8
PAST_LEN = 2048
PAGE_SIZE = 128
N_PAGES = PAST_LEN // PAGE_SIZE
W_A = D_MODEL
LRU_BLOCKS = 8
LRU_BLOCK_DIM = W_A // LRU_BLOCKS
CONV_WIDTH = 4
LRU_C = 8.0
B_HEADS = 8
B_HEAD_DIM = D_MODEL // B_HEADS
W_B = B_HEADS * B_HEAD_DIM
MOBA_BLOCK = 256
MOBA_TOPK = 3
N_BLOCKS = SEQ // MOBA_BLOCK
REL_BUCKETS = 32
REL_MAX_DIST = 128
X_HEADS = 4
X_HEAD_DIM = 128
W_X = X_HEADS * X_HEAD_DIM
MEM_LEN = 256
D_FF = 4 * D_MODEL
EPS = 1e-6

SUBLANES = 8
VMEM_LIMIT = 56 * 1024 * 1024
NEG_INF = float("-inf")


def _cparams(*sem):
    return pltpu.CompilerParams(dimension_semantics=sem, vmem_limit_bytes=VMEM_LIMIT)


def _dot(a, b):
    return jnp.dot(a, b, preferred_element_type=F32)


def _dot_nt(a, b, precision=None):
    return lax.dot_general(a, b, (((1,), (1,)), ((), ())), preferred_element_type=F32, precision=precision)


def _rms(x, g):
    return x * lax.rsqrt(jnp.mean(x * x, axis=-1, keepdims=True) + EPS) * g


def _t5_thresholds():
    max_exact = REL_BUCKETS // 2
    thr = []
    for k in range(1, REL_BUCKETS - max_exact):
        d = max_exact
        while int(math.log(d / max_exact) / math.log(REL_MAX_DIST / max_exact) * (REL_BUCKETS - max_exact)) < k:
            d += 1
        thr.append(d)
    return thr


T5_THRESHOLDS = _t5_thresholds()


def _mem_kv_kernel(x_ref, g_ref, w_ref, k_ref, v_ref):
    hn = _rms(x_ref[...], g_ref[...]).astype(BF16)
    r = _dot(hn, w_ref[...])
    k_ref[...] = r[:, :W_X]
    v_ref[...] = r[:, W_X:]


def _mem_kv(mem2d, norm_mem, w_bf):
    m = mem2d.shape[0]
    tm = 512
    return pl.pallas_call(
        _mem_kv_kernel,
        grid=(DEPTH, m // tm),
        in_specs=[
            pl.BlockSpec((tm, D_MODEL), lambda l, i: (i, 0)),
            pl.BlockSpec((None, 1, D_MODEL), lambda l, i: (l, 0, 0)),
            pl.BlockSpec((None, D_MODEL, 2 * W_X), lambda l, i: (l, 0, 0)),
        ],
        out_specs=[pl.BlockSpec((None, tm, W_X), lambda l, i: (l, i, 0))] * 2,
        out_shape=[jax.ShapeDtypeStruct((DEPTH, m, W_X), F32)] * 2,
        compiler_params=_cparams("parallel", "parallel"),
        name="mem_kv",
    )(mem2d, norm_mem, w_bf)


LANES = 128


def _norm_proj_kernel(x_ref, g_ref, w_ref, *out_refs, outs):
    hn = _rms(x_ref[...], g_ref[...]).astype(BF16)
    tm = hn.shape[0]
    results = {}
    for o_ref, (off, n, kind, _) in zip(out_refs, outs):
        if off not in results:
            results[off] = _dot(hn, w_ref[:, off:off + n])
        r = results[off]
        if kind == "tok":
            o_ref[...] = r.astype(o_ref.dtype)
        elif kind == "rows":
            heads = n // LANES
            for h in range(heads):
                o_ref[pl.ds(h, tm, stride=heads), :] = r[:, h * LANES:(h + 1) * LANES]
        else:
            for i in range(tm // MOBA_BLOCK):
                o_ref[i] = jnp.mean(r[i * MOBA_BLOCK:(i + 1) * MOBA_BLOCK], axis=0, keepdims=True)


def _norm_proj(x, g, w_bf, outs):
    m = x.shape[0]
    tm = 512
    n_all = w_bf.shape[1]
    specs, shapes = [], []
    for _, n, kind, dt in outs:
        if kind == "tok":
            specs.append(pl.BlockSpec((tm, n), lambda i: (i, 0)))
            shapes.append(jax.ShapeDtypeStruct((m, n), dt))
        elif kind == "rows":
            heads = n // LANES
            specs.append(pl.BlockSpec((tm * heads, LANES), lambda i: (i, 0)))
            shapes.append(jax.ShapeDtypeStruct((m * heads, LANES), dt))
        else:
            specs.append(pl.BlockSpec((tm // MOBA_BLOCK, 1, n), lambda i: (i, 0, 0)))
            shapes.append(jax.ShapeDtypeStruct((m // MOBA_BLOCK, 1, n), dt))
    return pl.pallas_call(
        functools.partial(_norm_proj_kernel, outs=outs),
        grid=(m // tm,),
        in_specs=[
            pl.BlockSpec((tm, D_MODEL), lambda i: (i, 0)),
            pl.BlockSpec((1, D_MODEL), lambda i: (0, 0)),
            pl.BlockSpec((D_MODEL, n_all), lambda i: (0, 0)),
        ],
        out_specs=specs,
        out_shape=shapes,
        compiler_params=_cparams("parallel"),
        name="norm_proj",
    )(x, g, w_bf)


def _softmax_pv(s, v_bf):
    m = jnp.max(s, axis=-1, keepdims=True)
    p = jnp.exp(s - m)
    l = jnp.sum(p, axis=-1, keepdims=True)
    return _dot(p.astype(BF16), v_bf) / l


def _xattn_prompt_kernel(q_ref, k_ref, v_ref, o_ref):
    scale = X_HEAD_DIM ** -0.5
    for h in range(X_HEADS):
        sl = slice(h * X_HEAD_DIM, (h + 1) * X_HEAD_DIM)
        s = _dot_nt(q_ref[:, sl], k_ref[:, sl].astype(BF16)) * scale
        o_ref[:, sl] = _softmax_pv(s, v_ref[:, sl].astype(BF16)).astype(o_ref.dtype)


def _xattn_prompt(qx, memk, memv, layer):
    tq = 512
    nt = SEQ // tq
    kv_spec = pl.BlockSpec((None, MEM_LEN, W_X), lambda b, t: (layer, b, 0))
    return pl.pallas_call(
        _xattn_prompt_kernel,
        grid=(BATCH, nt),
        in_specs=[pl.BlockSpec((tq, W_X), lambda b, t: (b * nt + t, 0)), kv_spec, kv_spec],
        out_specs=pl.BlockSpec((tq, W_X), lambda b, t: (b * nt + t, 0)),
        out_shape=jax.ShapeDtypeStruct((BATCH * SEQ, W_X), BF16),
        compiler_params=_cparams("parallel", "parallel"),
        name="xattn_prompt",
    )(qx, memk, memv)


def _head_block_mask(rows, cols, rows_per_head, cols_per_head):
    r = lax.broadcasted_iota(jnp.int32, (rows, cols), 0) // rows_per_head
    c = lax.broadcasted_iota(jnp.int32, (rows, cols), 1) // cols_per_head
    return r == c


def _fold_heads(o, mask, n_heads):
    o = jnp.where(mask, o, 0.0)
    acc = o[0:DEC_SEQ]
    for h in range(1, n_heads):
        acc = acc + o[h * DEC_SEQ:(h + 1) * DEC_SEQ]
    return acc


def _xattn_sample_kernel(q_ref, k_ref, v_ref, o_ref, *, group):
    scale = X_HEAD_DIM ** -0.5
    mask = _head_block_mask(X_HEADS * DEC_SEQ, W_X, DEC_SEQ, X_HEAD_DIM)
    for g in range(group):
        rows = slice(g * DEC_SEQ, (g + 1) * DEC_SEQ)
        q = q_ref[rows, :]
        qbd = jnp.where(mask, jnp.concatenate([q] * X_HEADS, axis=0), 0.0).astype(BF16)
        s = _dot_nt(qbd, k_ref[g].astype(BF16)) * scale
        o = _softmax_pv(s, v_ref[g].astype(BF16))
        o_ref[rows, :] = _fold_heads(o, mask, X_HEADS)


def _xattn_sample(qx, cmk, cmv, layer):
    group = 8
    kv_spec = pl.BlockSpec((None, group, MEM_LEN, W_X), lambda i: (layer, i, 0, 0))
    return pl.pallas_call(
        functools.partial(_xattn_sample_kernel, group=group),
        grid=(DEC_BATCH // group,),
        in_specs=[pl.BlockSpec((group * DEC_SEQ, W_X), lambda i: (i, 0)), kv_spec, kv_spec],
        out_specs=pl.BlockSpec((group * DEC_SEQ, W_X), lambda i: (i, 0)),
        out_shape=jax.ShapeDtypeStruct((DEC_BATCH * DEC_SEQ, W_X), F32),
        compiler_params=_cparams("parallel"),
        name="xattn_sample",
    )(qx, cmk, cmv)


def _post_kernel(x_ref, y_ref, c_ref, wy_ref, wc_ref, g_ref, wu_ref, wd_ref, gf_ref, o_ref,
                 xn_scr, hn_scr, acc_scr, *, final_norm):
    f = pl.program_id(1)

    @pl.when(f == 0)
    def _():
        xn = (x_ref[...] + _dot(y_ref[...].astype(BF16), wy_ref[...])
              + _dot(c_ref[...].astype(BF16), wc_ref[...]))
        xn_scr[...] = xn
        hn_scr[...] = _rms(xn, g_ref[...]).astype(BF16)
        acc_scr[...] = jnp.zeros_like(acc_scr)

    h = _dot(hn_scr[...], wu_ref[...])
    h = jnp.square(jnp.maximum(h, 0.0)).astype(BF16)
    acc_scr[...] += _dot(h, wd_ref[...])

    @pl.when(f == pl.num_programs(1) - 1)
    def _():
        out = xn_scr[...] + acc_scr[...]
        if final_norm:
            out = _rms(out, gf_ref[...])
        o_ref[...] = out


def _post(x, y, ca, wy, wc, g, wu, wd, gf, final_norm):
    m = x.shape[0]
    tm, tf = 512, 1024
    row = lambda i, f: (i, 0)
    const = lambda i, f: (0, 0)
    return pl.pallas_call(
        functools.partial(_post_kernel, final_norm=final_norm),
        grid=(m // tm, D_FF // tf),
        in_specs=[
            pl.BlockSpec((tm, D_MODEL), row),
            pl.BlockSpec((tm, y.shape[1]), row),
            pl.BlockSpec((tm, W_X), row),
            pl.BlockSpec(wy.shape, const),
            pl.BlockSpec(wc.shape, const),
            pl.BlockSpec((1, D_MODEL), const),
            pl.BlockSpec((D_MODEL, tf), lambda i, f: (0, f)),
            pl.BlockSpec((tf, D_MODEL), lambda i, f: (f, 0)),
            pl.BlockSpec((1, D_MODEL), const),
        ],
        out_specs=pl.BlockSpec((tm, D_MODEL), row),
        out_shape=jax.ShapeDtypeStruct((m, D_MODEL), F32),
        scratch_shapes=[
            pltpu.VMEM((tm, D_MODEL), F32),
            pltpu.VMEM((tm, D_MODEL), BF16),
            pltpu.VMEM((tm, D_MODEL), F32),
        ],
        compiler_params=_cparams("parallel", "arbitrary"),
        name="post",
    )(x, y, ca, wy, wc, g, wu, wd, gf)


def _gelu_tanh(x):
    return x * (0.5 * (1.0 + jnp.tanh(math.sqrt(2.0 / math.pi) * (x + 0.044715 * (x * x * x)))))


def _conv_rows(u, u_prev, cw_ref, cb_ref):
    rows = u.shape[0]
    g = rows // SUBLANES
    u3 = u.reshape(g, SUBLANES, W_A)
    p3 = u_prev.reshape(g, SUBLANES, W_A)
    sub = lax.broadcasted_iota(jnp.int32, (g, SUBLANES, W_A), 1)
    xc = cb_ref[...] + cw_ref[CONV_WIDTH - 1:CONV_WIDTH, :] * u3
    for j in range(1, CONV_WIDTH):
        sh = jnp.where(sub >= j, pltpu.roll(u3, j, axis=1), pltpu.roll(p3, j, axis=1))
        xc = xc + cw_ref[CONV_WIDTH - 1 - j:CONV_WIDTH - j, :] * sh
    return xc.reshape(rows, W_A)


def _lru_coeffs(xc, wcat_ref, gab_ref, gxb_ref, lam_ref, first_row, a_scr, b_scr):
    z = -lam_ref[...]
    softplus = jnp.maximum(z, 0.0) + jnp.log1p(jnp.exp(-jnp.abs(z)))
    for n in range(LRU_BLOCKS):
        sl = slice(n * LRU_BLOCK_DIM, (n + 1) * LRU_BLOCK_DIM)
        xn = xc[:, sl]
        ra = _dot(xn.astype(BF16), wcat_ref[n])
        r = jax.nn.sigmoid(ra[:, :LRU_BLOCK_DIM] + gab_ref[:, sl])
        ig = jax.nn.sigmoid(ra[:, LRU_BLOCK_DIM:] + gxb_ref[:, sl])
        log_a = -LRU_C * r * softplus[:, sl]
        th = jnp.tanh(log_a)
        mult = jnp.sqrt(-2.0 * th / (1.0 - th))
        if first_row is not None:
            mult = jnp.where(first_row, 1.0, mult)
        a_scr[:, sl] = jnp.exp(log_a)
        b_scr[:, sl] = mult * ig * xn


def _scan_within_tiles(a, b):
    rows = a.shape[0]
    g = rows // SUBLANES
    a3 = a.reshape(g, SUBLANES, W_A)
    b3 = b.reshape(g, SUBLANES, W_A)
    sub = lax.broadcasted_iota(jnp.int32, (g, SUBLANES, W_A), 1)
    for s in (1, 2, 4):
        keep = sub >= s
        b3 = jnp.where(keep, a3 * pltpu.roll(b3, s, axis=1) + b3, b3)
        a3 = jnp.where(keep, a3 * pltpu.roll(a3, s, axis=1), a3)
    return a3.reshape(rows, W_A), b3.reshape(rows, W_A)


def _lru_prompt_kernel(u_ref, gate_ref, wcat_ref, cw_ref, cb_ref, gab_ref, gxb_ref, lam_ref,
                       y_ref, utail_ref, htail_ref, tail_scr, carry_scr, a_scr, b_scr, h_scr, *, tile):
    t = pl.program_id(1)

    @pl.when(t == 0)
    def _():
        tail_scr[...] = jnp.zeros_like(tail_scr)
        carry_scr[...] = jnp.zeros_like(carry_scr)

    u = u_ref[...]
    u_prev = jnp.concatenate([tail_scr[...], u[:tile - SUBLANES]], axis=0)
    tail_scr[...] = u[tile - SUBLANES:]
    xc = _conv_rows(u, u_prev, cw_ref, cb_ref)
    row = lax.broadcasted_iota(jnp.int32, (tile, LRU_BLOCK_DIM), 0)
    first_row = jnp.logical_and(row == 0, t == 0)
    _lru_coeffs(xc, wcat_ref, gab_ref, gxb_ref, lam_ref, first_row, a_scr, b_scr)
    a, b = _scan_within_tiles(a_scr[...], b_scr[...])
    a_scr[...] = a
    b_scr[...] = b

    def step(g, carry):
        rows = pl.ds(pl.multiple_of(g * SUBLANES, SUBLANES), SUBLANES)
        h = b_scr[rows, :] + a_scr[rows, :] * carry
        h_scr[rows, :] = h
        return jnp.broadcast_to(h[SUBLANES - 1:SUBLANES, :], (SUBLANES, W_A))

    carry_scr[...] = lax.fori_loop(0, tile // SUBLANES, step, carry_scr[...])
    y_ref[...] = (_gelu_tanh(gate_ref[...]) * h_scr[...]).astype(y_ref.dtype)
    utail_ref[...] = u[tile - SUBLANES:]
    htail_ref[...] = h_scr[tile - SUBLANES:, :]


def _lru_prompt(u, gate, wcat, cw, cb, gab, gxb, lam):
    tile = 256
    nt = SEQ // tile
    row = lambda b, t: (b * nt + t, 0)
    const2 = lambda b, t: (0, 0)
    vec = pl.BlockSpec((1, W_A), const2)
    tail_spec = pl.BlockSpec((None, SUBLANES, W_A), lambda b, t: (b, 0, 0))
    return pl.pallas_call(
        functools.partial(_lru_prompt_kernel, tile=tile),
        grid=(BATCH, nt),
        in_specs=[
            pl.BlockSpec((tile, W_A), row),
            pl.BlockSpec((tile, W_A), row),
            pl.BlockSpec(wcat.shape, lambda b, t: (0, 0, 0)),
            pl.BlockSpec((CONV_WIDTH, W_A), const2),
            vec, vec, vec, vec,
        ],
        out_specs=[pl.BlockSpec((tile, W_A), row), tail_spec, tail_spec],
        out_shape=[
            jax.ShapeDtypeStruct((BATCH * SEQ, W_A), BF16),
            jax.ShapeDtypeStruct((BATCH, SUBLANES, W_A), F32),
            jax.ShapeDtypeStruct((BATCH, SUBLANES, W_A), F32),
        ],
        scratch_shapes=[
            pltpu.VMEM((SUBLANES, W_A), F32),
            pltpu.VMEM((SUBLANES, W_A), F32),
            pltpu.VMEM((tile, W_A), F32),
            pltpu.VMEM((tile, W_A), F32),
            pltpu.VMEM((tile, W_A), F32),
        ],
        compiler_params=_cparams("parallel", "arbitrary"),
        name="lru_prompt",
    )(u, gate, wcat, cw, cb, gab, gxb, lam)


def _lru_sample_kernel(u_ref, gate_ref, prev_ref, h0_ref, wcat_ref, cw_ref, cb_ref, gab_ref, gxb_ref, lam_ref,
                       y_ref, h_ref, a_scr, b_scr):
    xc = _conv_rows(u_ref[...], prev_ref[...], cw_ref, cb_ref)
    _lru_coeffs(xc, wcat_ref, gab_ref, gxb_ref, lam_ref, None, a_scr, b_scr)
    a = a_scr[...]
    _, h = _scan_within_tiles(a, b_scr[...] + a * h0_ref[...])
    h_ref[...] = h
    y_ref[...] = (_gelu_tanh(gate_ref[...]) * h).astype(y_ref.dtype)


def _lru_sample(u, gate, prev, h0pad, wcat, cw, cb, gab, gxb, lam):
    m = u.shape[0]
    tile = 256
    row = lambda i: (i, 0)
    const2 = lambda i: (0, 0)
    vec = pl.BlockSpec((1, W_A), const2)
    blk = pl.BlockSpec((tile, W_A), row)
    return pl.pallas_call(
        _lru_sample_kernel,
        grid=(m // tile,),
        in_specs=[blk, blk, blk, blk,
                  pl.BlockSpec(wcat.shape, lambda i: (0, 0, 0)),
                  pl.BlockSpec((CONV_WIDTH, W_A), const2),
                  vec, vec, vec, vec],
        out_specs=[blk, blk],
        out_shape=[jax.ShapeDtypeStruct((m, W_A), BF16), jax.ShapeDtypeStruct((m, W_A), F32)],
        scratch_shapes=[pltpu.VMEM((tile, W_A), F32), pltpu.VMEM((tile, W_A), F32)],
        compiler_params=_cparams("parallel"),
        name="lru_sample",
    )(u, gate, prev, h0pad, wcat, cw, cb, gab, gxb, lam)


def _bias_tile_kernel(rel_ref, o_ref):
    h = pl.program_id(0)
    w = pl.program_id(1)
    shape = (MOBA_BLOCK, MOBA_BLOCK)
    r = lax.broadcasted_iota(jnp.int32, shape, 0)
    c = lax.broadcasted_iota(jnp.int32, shape, 1)
    dist = jnp.maximum(w * MOBA_BLOCK + c - r, 0)
    large = jnp.full(shape, REL_BUCKETS // 2, jnp.int32)
    for thr in T5_THRESHOLDS:
        large = large + (dist >= thr).astype(jnp.int32)
    bucket = jnp.where(dist < REL_BUCKETS // 2, dist, large)
    out = jnp.zeros(shape, F32)
    for b in range(REL_BUCKETS):
        out = jnp.where(bucket == b, rel_ref[b, h], out)
    o_ref[...] = out


def _bias_tiles(rel_bias):
    return pl.pallas_call(
        _bias_tile_kernel,
        grid=(B_HEADS, 2),
        in_specs=[pl.BlockSpec(memory_space=pltpu.SMEM)],
        out_specs=pl.BlockSpec((None, None, MOBA_BLOCK, MOBA_BLOCK), lambda h, w: (h, w, 0, 0)),
        out_shape=jax.ShapeDtypeStruct((B_HEADS, 2, MOBA_BLOCK, MOBA_BLOCK), F32),
        compiler_params=_cparams("parallel", "parallel"),
        name="bias_tiles",
    )(rel_bias)


def _select_topk(gate, n_past):
    blk = lax.broadcasted_iota(jnp.int32, gate.shape, 1)
    past = blk < n_past
    g = jnp.where(past, gate, NEG_INF)
    rank = jnp.zeros(gate.shape, jnp.int32)
    for m in range(gate.shape[1]):
        gm = g[:, m:m + 1]
        beats = jnp.logical_or(gm > g, jnp.logical_and(gm == g, blk > m))
        rank = rank + beats.astype(jnp.int32)
    return jnp.where(jnp.logical_and(past, rank < MOBA_TOPK), 1.0, 0.0)


def _select_topk_t(gate_t, n_past):
    blk = lax.broadcasted_iota(jnp.int32, gate_t.shape, 0)
    past = blk < n_past
    g = jnp.where(past, gate_t, NEG_INF)
    rank = jnp.zeros(gate_t.shape, jnp.int32)
    for m in range(gate_t.shape[0]):
        gm = g[m:m + 1, :]
        beats = jnp.logical_or(gm > g, jnp.logical_and(gm == g, blk > m))
        rank = rank + beats.astype(jnp.int32)
    return jnp.where(jnp.logical_and(past, rank < MOBA_TOPK), 1.0, 0.0)


def _moba_prompt_kernel(rel_ref, q_ref, k_ref, v_ref, kmean_ref, bias_ref, o_ref, vt_scr, s_scr):
    h = pl.program_id(1)
    i = pl.program_id(2)
    scale = B_HEAD_DIM ** -0.5

    @pl.when(i == 0)
    def _():
        for n in range(N_BLOCKS):
            cols = slice(n * MOBA_BLOCK, (n + 1) * MOBA_BLOCK)
            vt_scr[:, cols] = v_ref[cols, :].astype(F32).T.astype(BF16)

    q = q_ref[...]
    qb = q.astype(BF16)
    gate_t = _dot_nt(kmean_ref[...], q, precision=lax.Precision.HIGHEST)
    penalty_t = jnp.where(_select_topk_t(gate_t, i) > 0.0, 0.0, NEG_INF)
    far_bias = rel_ref[REL_BUCKETS - 1, h]
    key = lax.broadcasted_iota(jnp.int32, (MOBA_BLOCK, MOBA_BLOCK), 0)
    query = lax.broadcasted_iota(jnp.int32, (MOBA_BLOCK, MOBA_BLOCK), 1)

    def tile_rows(x, op):
        return op(x.reshape(MOBA_BLOCK // SUBLANES, SUBLANES, MOBA_BLOCK), axis=0)

    def attend(own):
        m_part = None
        for n in range(own, -1, -1):
            rows = slice(n * MOBA_BLOCK, (n + 1) * MOBA_BLOCK)
            s = _dot_nt(k_ref[rows, :], qb) * scale
            if n == own:
                s = jnp.where(key <= query, s + bias_ref[0], NEG_INF)
            elif n == own - 1:
                s = s + bias_ref[1] + penalty_t[n:n + 1, :]
            else:
                s = s + (far_bias + penalty_t[n:n + 1, :])
            s_scr[n] = s
            part = tile_rows(s, jnp.max)
            m_part = part if m_part is None else jnp.maximum(m_part, part)
        m = jnp.max(m_part, axis=0, keepdims=True)
        l_part = jnp.zeros((SUBLANES, MOBA_BLOCK), F32)
        acc = jnp.zeros((B_HEAD_DIM, MOBA_BLOCK), F32)
        for n in range(own + 1):
            rows = slice(n * MOBA_BLOCK, (n + 1) * MOBA_BLOCK)
            p = jnp.exp(s_scr[n] - m)
            l_part = l_part + tile_rows(p, jnp.sum)
            acc = acc + _dot(vt_scr[:, rows], p.astype(BF16))
        l = jnp.sum(l_part, axis=0, keepdims=True)
        o_ref[...] = (acc / l).T.astype(o_ref.dtype)

    for own in range(N_BLOCKS):
        pl.when(i == own)(functools.partial(attend, own))


def _moba_prompt(q, k_bf, v_bf, kmean, bias_tiles, rel_bias):
    batch = q.shape[0] // SEQ
    kv_spec = pl.BlockSpec((SEQ, B_HEAD_DIM), lambda b, h, i: (b, h))
    q_spec = pl.BlockSpec((MOBA_BLOCK, B_HEAD_DIM), lambda b, h, i: (b * N_BLOCKS + i, h))
    return pl.pallas_call(
        _moba_prompt_kernel,
        grid=(batch, B_HEADS, N_BLOCKS),
        in_specs=[
            pl.BlockSpec(memory_space=pltpu.SMEM),
            q_spec, kv_spec, kv_spec,
            pl.BlockSpec((None, N_BLOCKS, B_HEAD_DIM), lambda b, h, i: (b, 0, h)),
            pl.BlockSpec((None, 2, MOBA_BLOCK, MOBA_BLOCK), lambda b, h, i: (h, 0, 0, 0)),
        ],
        out_specs=q_spec,
        out_shape=jax.ShapeDtypeStruct((batch * SEQ, W_B), BF16),
        scratch_shapes=[pltpu.VMEM((B_HEAD_DIM, SEQ), BF16),
                        pltpu.VMEM((N_BLOCKS, MOBA_BLOCK, MOBA_BLOCK), F32)],
        compiler_params=_cparams("parallel", "parallel", "arbitrary"),
        name="moba_prompt",
    )(rel_bias, q, k_bf, v_bf, kmean, bias_tiles)


SAMPLE_ROWS = DEC_SEQ * B_HEADS
PAGE_ROWS = PAGE_SIZE * B_HEADS


def _moba_sample_kernel(pt_ref, q_ref, kn_ref, vn_ref, bprev_ref, bnew_ref, far_ref, *refs):
    del pt_ref
    k_pages = refs[:N_PAGES]
    v_pages = refs[N_PAGES:2 * N_PAGES]
    o_ref = refs[2 * N_PAGES]
    s_scr = refs[2 * N_PAGES + 1]
    scale = B_HEAD_DIM ** -0.5
    pages_per_block = MOBA_BLOCK // PAGE_SIZE
    n_past = PAST_LEN // MOBA_BLOCK

    q = q_ref[...]
    qb = q.astype(BF16)
    q3 = q.reshape(DEC_SEQ, B_HEADS, B_HEAD_DIM)
    gate_cols = []
    for n in range(n_past):
        ksum = sum(k_pages[n * pages_per_block + j][...].reshape(PAGE_SIZE, B_HEADS, B_HEAD_DIM).sum(axis=0)
                   for j in range(pages_per_block))
        kmean = ksum * (1.0 / MOBA_BLOCK)
        gate_cols.append(jnp.sum(q3 * kmean[None], axis=-1, keepdims=True).reshape(SAMPLE_ROWS, 1))
    gate = jnp.concatenate(gate_cols, axis=1)
    penalty = jnp.where(_select_topk(gate, n_past) > 0.0, 0.0, NEG_INF)

    def same_head_penalty(cols):
        r = lax.broadcasted_iota(jnp.int32, (SAMPLE_ROWS, cols), 0) % B_HEADS
        c = lax.broadcasted_iota(jnp.int32, (SAMPLE_ROWS, cols), 1) % B_HEADS
        return jnp.where(r == c, 0.0, NEG_INF)

    head_pen = same_head_penalty(PAGE_ROWS)
    far_bias = far_ref[...]
    m_el = jnp.full((SAMPLE_ROWS, PAGE_ROWS), NEG_INF, F32)
    for p in range(N_PAGES):
        n = p // pages_per_block
        s = _dot_nt(qb, k_pages[p][...].astype(BF16)) * scale
        if n == n_past - 1:
            j = p - n * pages_per_block
            s = s + bprev_ref[:, j * PAGE_ROWS:(j + 1) * PAGE_ROWS] + penalty[:, n:n + 1]
        else:
            s = s + (far_bias + penalty[:, n:n + 1])
        s = s + head_pen
        s_scr[p] = s
        m_el = jnp.maximum(m_el, s)

    s_new = _dot_nt(q, kn_ref[...]) * scale + bnew_ref[...] + same_head_penalty(SAMPLE_ROWS)
    t_q = lax.broadcasted_iota(jnp.int32, (SAMPLE_ROWS, SAMPLE_ROWS), 0) // B_HEADS
    t_k = lax.broadcasted_iota(jnp.int32, (SAMPLE_ROWS, SAMPLE_ROWS), 1) // B_HEADS
    s_new = jnp.where(t_k <= t_q, s_new, NEG_INF)

    m = jnp.maximum(jnp.max(m_el, axis=-1, keepdims=True), jnp.max(s_new, axis=-1, keepdims=True))
    p_new = jnp.exp(s_new - m)
    acc = _dot(p_new, vn_ref[...])
    l_el = jnp.zeros((SAMPLE_ROWS, PAGE_ROWS), F32)
    for p in range(N_PAGES):
        pr = jnp.exp(s_scr[p] - m)
        l_el = l_el + pr
        acc = acc + _dot(pr.astype(BF16), v_pages[p][...].astype(BF16))
    l = jnp.sum(p_new, axis=-1, keepdims=True) + jnp.sum(l_el, axis=-1, keepdims=True)
    o_ref[...] = acc / l


def _moba_sample(q_rows, k_rows, v_rows, cache_k_rows, cache_v_rows, layer, page_table_flat,
                 bias_prev, bias_new, far_col):
    n_seq = q_rows.shape[0] // SAMPLE_ROWS
    tok_spec = pl.BlockSpec((SAMPLE_ROWS, B_HEAD_DIM), lambda b, pt: (b, 0))

    def page_spec(p):
        return pl.BlockSpec((None, None, PAGE_ROWS, B_HEAD_DIM),
                            lambda b, pt: (layer, pt[b * N_PAGES + p], 0, 0))

    def const_spec(a):
        return pl.BlockSpec(a.shape, lambda b, pt: (0, 0))

    pages = [page_spec(p) for p in range(N_PAGES)]
    grid_spec = pltpu.PrefetchScalarGridSpec(
        num_scalar_prefetch=1,
        grid=(n_seq,),
        in_specs=[tok_spec, tok_spec, tok_spec, const_spec(bias_prev), const_spec(bias_new),
                  const_spec(far_col)] + pages + pages,
        out_specs=tok_spec,
        scratch_shapes=[pltpu.VMEM((N_PAGES, SAMPLE_ROWS, PAGE_ROWS), F32)],
    )
    return pl.pallas_call(
        _moba_sample_kernel,
        grid_spec=grid_spec,
        out_shape=jax.ShapeDtypeStruct((n_seq * SAMPLE_ROWS, B_HEAD_DIM), F32),
        compiler_params=_cparams("parallel"),
        name="moba_sample",
    )(page_table_flat, q_rows, k_rows, v_rows, bias_prev, bias_new, far_col,
      *([cache_k_rows] * N_PAGES), *([cache_v_rows] * N_PAGES))


def kernel(x_prompt, x_sample, cache_k, cache_v, state_conv, state_h, cache_mem_k, cache_mem_v, page_table,
           mem_prompt, norm_mix, norm_mlp, norm_mem, norm_final, w_in_a, w_out_a, conv_w, conv_b, gate_a_w,
           gate_a_b, gate_x_w, gate_x_b, lru_lambda, w_in_b, w_out_b, rel_bias, w_mem_kv, w_up, w_down):
    n_prompt = BATCH * SEQ
    n_sample = DEC_BATCH * DEC_SEQ
    xp = x_prompt.reshape(n_prompt, D_MODEL)
    xs = x_sample.reshape(n_sample, D_MODEL)

    w_in_a_bf, w_in_b_bf = w_in_a.astype(BF16), w_in_b.astype(BF16)
    w_out_a_bf, w_out_b_bf = w_out_a.astype(BF16), w_out_b.astype(BF16)
    w_up_bf, w_down_bf = w_up.astype(BF16), w_down.astype(BF16)
    wcat_bf = jnp.concatenate([gate_a_w, gate_x_w], axis=-1).astype(BF16)
    gf = norm_final.reshape(1, D_MODEL)

    memk, memv = _mem_kv(mem_prompt.reshape(BATCH * MEM_LEN, D_MODEL), norm_mem.reshape(DEPTH, 1, D_MODEL),
                         w_mem_kv.astype(BF16))
    cmk = cache_mem_k.reshape(DEPTH, DEC_BATCH, MEM_LEN, W_X)
    cmv = cache_mem_v.reshape(DEPTH, DEC_BATCH, MEM_LEN, W_X)
    cache_k_rows = cache_k.reshape(cache_k.shape[0], cache_k.shape[1], PAGE_ROWS, B_HEAD_DIM)
    cache_v_rows = cache_v.reshape(cache_v.shape[0], cache_v.shape[1], PAGE_ROWS, B_HEAD_DIM)
    pt_flat = page_table.reshape(DEC_BATCH * N_PAGES)

    tiles = _bias_tiles(rel_bias)
    bias_prev = jnp.transpose(tiles[:, 1, :, :DEC_SEQ], (2, 1, 0)).reshape(DEC_SEQ, 1, MOBA_BLOCK * B_HEADS)
    bias_prev = jnp.broadcast_to(bias_prev, (DEC_SEQ, B_HEADS, MOBA_BLOCK * B_HEADS))
    bias_prev = bias_prev.reshape(SAMPLE_ROWS, MOBA_BLOCK * B_HEADS)
    bias_new = jnp.transpose(tiles[:, 0, :DEC_SEQ, :DEC_SEQ], (2, 1, 0)).reshape(DEC_SEQ, 1, SAMPLE_ROWS)
    bias_new = jnp.broadcast_to(bias_new, (DEC_SEQ, B_HEADS, SAMPLE_ROWS)).reshape(SAMPLE_ROWS, SAMPLE_ROWS)
    far_col = jnp.tile(rel_bias[REL_BUCKETS - 1], DEC_SEQ).reshape(SAMPLE_ROWS, 1)

    kp_l, vp_l, ks_l, vs_l = [], [], [], []
    cp_l, hp_l, cs_l, hs_l = [], [], [], []
    for i in range(DEPTH):
        j = i // 2
        g_mix = norm_mix[i].reshape(1, D_MODEL)
        g_mlp = norm_mlp[i].reshape(1, D_MODEL)
        final = i == DEPTH - 1
        if i % 2 == 0:
            lru_w = (wcat_bf[j], conv_w[j], conv_b[j].reshape(1, W_A), gate_a_b[j].reshape(1, W_A),
                     gate_x_b[j].reshape(1, W_A), lru_lambda[j].reshape(1, W_A))
            w_out = w_out_a_bf[j]
            mix_w = W_A

            def outs_a(qx_dtype):
                return ((0, W_A, "tok", F32), (W_A, W_A, "tok", F32), (2 * W_A, W_X, "tok", qx_dtype))

            u, gate, qx = _norm_proj(xp, g_mix, w_in_a_bf[j], outs_a(BF16))
            y_p, utail, htail = _lru_prompt(u, gate, *lru_w)
            cp_l.append(utail[:, SUBLANES - (CONV_WIDTH - 1):, :])
            hp_l.append(htail[:, SUBLANES - 1, :])

            u, gate, qx_s = _norm_proj(xs, g_mix, w_in_a_bf[j], outs_a(F32))
            prev = jnp.pad(state_conv[j], ((0, 0), (SUBLANES - (CONV_WIDTH - 1), 0), (0, 0)))
            h0pad = jnp.pad(state_h[j][:, None, :], ((0, 0), (0, DEC_SEQ - 1), (0, 0)))
            y_s, h_s = _lru_sample(u, gate, prev.reshape(n_sample, W_A), h0pad.reshape(n_sample, W_A), *lru_w)
            cs_l.append(u.reshape(DEC_BATCH, DEC_SEQ, W_A)[:, DEC_SEQ - (CONV_WIDTH - 1):, :])
            hs_l.append(h_s.reshape(DEC_BATCH, DEC_SEQ, W_A)[:, DEC_SEQ - 1, :])
        else:
            w_out = w_out_b_bf[j]
            mix_w = W_B

            outs_p = ((0, W_B, "tok", F32),
                      (W_B, W_B, "rows", F32), (W_B, W_B, "tok", BF16), (W_B, W_B, "blockmean", F32),
                      (2 * W_B, W_B, "rows", F32), (2 * W_B, W_B, "tok", BF16),
                      (3 * W_B, W_X, "tok", BF16))
            q, k_rows, k_bf, kmean, v_rows, v_bf, qx = _norm_proj(xp, g_mix, w_in_b_bf[j], outs_p)
            y_p = _moba_prompt(q, k_bf, v_bf, kmean.reshape(BATCH, N_BLOCKS, W_B), tiles, rel_bias)
            kp_l.append(k_rows.reshape(BATCH, SEQ, B_HEADS, B_HEAD_DIM))
            vp_l.append(v_rows.reshape(BATCH, SEQ, B_HEADS, B_HEAD_DIM))

            outs_s = ((0, W_B, "rows", F32), (W_B, W_B, "rows", F32), (2 * W_B, W_B, "rows", F32),
                      (3 * W_B, W_X, "tok", F32))
            q_rows, k_rows, v_rows, qx_s = _norm_proj(xs, g_mix, w_in_b_bf[j], outs_s)
            y_s = _moba_sample(q_rows, k_rows, v_rows, cache_k_rows, cache_v_rows, j, pt_flat,
                               bias_prev, bias_new, far_col).reshape(n_sample, W_B)
            ks_l.append(k_rows.reshape(DEC_BATCH, DEC_SEQ, B_HEADS, B_HEAD_DIM))
            vs_l.append(v_rows.reshape(DEC_BATCH, DEC_SEQ, B_HEADS, B_HEAD_DIM))

        ca_p = _xattn_prompt(qx, memk, memv, i)
        ca_s = _xattn_sample(qx_s, cmk, cmv, i)
        post_w = (w_out[:mix_w], w_out[mix_w:], g_mlp, w_up_bf[i], w_down_bf[i], gf, final)
        xp = _post(xp, y_p, ca_p, *post_w)
        xs = _post(xs, y_s, ca_s, *post_w)

    mem_shape = (DEPTH, BATCH, MEM_LEN, X_HEADS, X_HEAD_DIM)
    return (xp.reshape(BATCH, SEQ, D_MODEL), xs.reshape(DEC_BATCH, DEC_SEQ, D_MODEL),
            jnp.stack(kp_l), jnp.stack(vp_l), jnp.stack(ks_l), jnp.stack(vs_l),
            jnp.stack(cp_l), jnp.stack(hp_l), jnp.stack(cs_l), jnp.stack(hs_l),
            memk.reshape(mem_shape), memv.reshape(mem_shape))
```

```python
import functools
import math

import jax
import jax.numpy as jnp
from jax import lax
from jax.experimental import pallas as pl
from jax.experimental.pallas import tpu as pltpu

F32 = jnp.float32
BF16 = jnp.bfloat16

D_MODEL = 1024
BATCH = 8
SEQ = 2048
DEPTH = 4
DEC_BATCH = 128
DEC_SEQ = 8
PAST_LEN = 2048
PAGE_SIZE = 128
N_PAGES = PAST_LEN // PAGE_SIZE
W_A = D_MODEL
LRU_BLOCKS = 8
LRU_BLOCK_DIM = W_A // LRU_BLOCKS
CONV_WIDTH = 4
LRU_C = 8.0
B_HEADS = 8
B_HEAD_DIM = D_MODEL // B_HEADS
W_B = B_HEADS * B_HEAD_DIM
MOBA_BLOCK = 256
MOBA_TOPK = 3
N_BLOCKS = SEQ // MOBA_BLOCK
REL_BUCKETS = 32
REL_MAX_DIST = 128
X_HEADS = 4
X_HEAD_DIM = 128
W_X = X_HEADS * X_HEAD_DIM
MEM_LEN = 256
D_FF = 4 * D_MODEL
EPS = 1e-6

SUBLANES = 8
VMEM_LIMIT = 56 * 1024 * 1024
NEG_INF = float("-inf")


def _cparams(*sem):
    return pltpu.CompilerParams(dimension_semantics=sem, vmem_limit_bytes=VMEM_LIMIT)


def _dot(a, b):
    return jnp.dot(a, b, preferred_element_type=F32)


def _dot_nt(a, b, precision=None):
    return lax.dot_general(a, b, (((1,), (1,)), ((), ())), preferred_element_type=F32, precision=precision)


def _rms(x, g):
    return x * lax.rsqrt(jnp.mean(x * x, axis=-1, keepdims=True) + EPS) * g


def _t5_thresholds():
    max_exact = REL_BUCKETS // 2
    thr = []
    for k in range(1, REL_BUCKETS - max_exact):
        d = max_exact
        while int(math.log(d / max_exact) / math.log(REL_MAX_DIST / max_exact) * (REL_BUCKETS - max_exact)) < k:
            d += 1
        thr.append(d)
    return thr


T5_THRESHOLDS = _t5_thresholds()


def _mem_kv_kernel(x_ref, g_ref, w_ref, k_ref, v_ref):
    hn = _rms(x_ref[...], g_ref[...]).astype(BF16)
    r = _dot(hn, w_ref[...])
    k_ref[...] = r[:, :W_X]
    v_ref[...] = r[:, W_X:]


def _mem_kv(mem2d, norm_mem, w_bf):
    m = mem2d.shape[0]
    tm = 512
    return pl.pallas_call(
        _mem_kv_kernel,
        grid=(DEPTH, m // tm),
        in_specs=[
            pl.BlockSpec((tm, D_MODEL), lambda l, i: (i, 0)),
            pl.BlockSpec((None, 1, D_MODEL), lambda l, i: (l, 0, 0)),
            pl.BlockSpec((None, D_MODEL, 2 * W_X), lambda l, i: (l, 0, 0)),
        ],
        out_specs=[pl.BlockSpec((None, tm, W_X), lambda l, i: (l, i, 0))] * 2,
        out_shape=[jax.ShapeDtypeStruct((DEPTH, m, W_X), F32)] * 2,
        compiler_params=_cparams("parallel", "parallel"),
        name="mem_kv",
    )(mem2d, norm_mem, w_bf)


LANES = 128


def _norm_proj_kernel(x_ref, g_ref, w_ref, *out_refs, outs):
    hn = _rms(x_ref[...], g_ref[...]).astype(BF16)
    tm = hn.shape[0]
    results = {}
    for o_ref, (off, n, kind, _) in zip(out_refs, outs):
        if off not in results:
            results[off] = _dot(hn, w_ref[:, off:off + n])
        r = results[off]
        if kind == "tok":
            o_ref[...] = r.astype(o_ref.dtype)
        elif kind == "rows":
            heads = n // LANES
            for h in range(heads):
                o_ref[pl.ds(h, tm, stride=heads), :] = r[:, h * LANES:(h + 1) * LANES]
        else:
            for i in range(tm // MOBA_BLOCK):
                o_ref[i] = jnp.mean(r[i * MOBA_BLOCK:(i + 1) * MOBA_BLOCK], axis=0, keepdims=True)


def _norm_proj(x, g, w_bf, outs):
    m = x.shape[0]
    tm = 512
    n_all = w_bf.shape[1]
    specs, shapes = [], []
    for _, n, kind, dt in outs:
        if kind == "tok":
            specs.append(pl.BlockSpec((tm, n), lambda i: (i, 0)))
            shapes.append(jax.ShapeDtypeStruct((m, n), dt))
        elif kind == "rows":
            heads = n // LANES
            specs.append(pl.BlockSpec((tm * heads, LANES), lambda i: (i, 0)))
            shapes.append(jax.ShapeDtypeStruct((m * heads, LANES), dt))
        else:
            specs.append(pl.BlockSpec((tm // MOBA_BLOCK, 1, n), lambda i: (i, 0, 0)))
            shapes.append(jax.ShapeDtypeStruct((m // MOBA_BLOCK, 1, n), dt))
    return pl.pallas_call(
        functools.partial(_norm_proj_kernel, outs=outs),
        grid=(m // tm,),
        in_specs=[
            pl.BlockSpec((tm, D_MODEL), lambda i: (i, 0)),
            pl.BlockSpec((1, D_MODEL), lambda i: (0, 0)),
            pl.BlockSpec((D_MODEL, n_all), lambda i: (0, 0)),
        ],
        out_specs=specs,
        out_shape=shapes,
        compiler_params=_cparams("parallel"),
        name="norm_proj",
    )(x, g, w_bf)


def _softmax_pv(s, v_bf):
    m = jnp.max(s, axis=-1, keepdims=True)
    p = jnp.exp(s - m)
    l = jnp.sum(p, axis=-1, keepdims=True)
    return _dot(p.astype(BF16), v_bf) / l


def _xattn_prompt_kernel(q_ref, k_ref, v_ref, o_ref):
    scale = X_HEAD_DIM ** -0.5
    for h in range(X_HEADS):
        sl = slice(h * X_HEAD_DIM, (h + 1) * X_HEAD_DIM)
        s = _dot_nt(q_ref[:, sl], k_ref[:, sl].astype(BF16)) * scale
        o_ref[:, sl] = _softmax_pv(s, v_ref[:, sl].astype(BF16)).astype(o_ref.dtype)


def _xattn_prompt(qx, memk, memv, layer):
    tq = 512
    nt = SEQ // tq
    kv_spec = pl.BlockSpec((None, MEM_LEN, W_X), lambda b, t: (layer, b, 0))
    return pl.pallas_call(
        _xattn_prompt_kernel,
        grid=(BATCH, nt),
        in_specs=[pl.BlockSpec((tq, W_X), lambda b, t: (b * nt + t, 0)), kv_spec, kv_spec],
        out_specs=pl.BlockSpec((tq, W_X), lambda b, t: (b * nt + t, 0)),
        out_shape=jax.ShapeDtypeStruct((BATCH * SEQ, W_X), BF16),
        compiler_params=_cparams("parallel", "parallel"),
        name="xattn_prompt",
    )(qx, memk, memv)


XQ_ROWS = DEC_SEQ * X_HEADS
XMEM_ROWS = MEM_LEN * X_HEADS


def _xattn_sample_kernel(q_ref, k_ref, v_ref, o_ref, *, group):
    scale = X_HEAD_DIM ** -0.5
    r = lax.broadcasted_iota(jnp.int32, (XQ_ROWS, XMEM_ROWS), 0) % X_HEADS
    c = lax.broadcasted_iota(jnp.int32, (XQ_ROWS, XMEM_ROWS), 1) % X_HEADS
    head_pen = jnp.where(r == c, 0.0, NEG_INF)
    for g in range(group):
        rows = slice(g * XQ_ROWS, (g + 1) * XQ_ROWS)
        s = _dot_nt(q_ref[rows, :].astype(BF16), k_ref[g].astype(BF16)) * scale + head_pen
        o_ref[rows, :] = _softmax_pv(s, v_ref[g].astype(BF16))


def _xattn_sample(q_rows, cmk_rows, cmv_rows, layer):
    group = 8
    n_seq = q_rows.shape[0] // XQ_ROWS
    kv_spec = pl.BlockSpec((None, group, XMEM_ROWS, X_HEAD_DIM), lambda i: (layer, i, 0, 0))
    q_spec = pl.BlockSpec((group * XQ_ROWS, X_HEAD_DIM), lambda i: (i, 0))
    return pl.pallas_call(
        functools.partial(_xattn_sample_kernel, group=group),
        grid=(n_seq // group,),
        in_specs=[q_spec, kv_spec, kv_spec],
        out_specs=q_spec,
        out_shape=jax.ShapeDtypeStruct((n_seq * XQ_ROWS, X_HEAD_DIM), F32),
        compiler_params=_cparams("parallel"),
        name="xattn_sample",
    )(q_rows, cmk_rows, cmv_rows)


def _post_kernel(x_ref, y_ref, c_ref, wy_ref, wc_ref, g_ref, wu_ref, wd_ref, gf_ref, o_ref,
                 xn_scr, hn_scr, acc_scr, *, final_norm):
    f = pl.program_id(1)

    @pl.when(f == 0)
    def _():
        xn = (x_ref[...] + _dot(y_ref[...].astype(BF16), wy_ref[...])
              + _dot(c_ref[...].astype(BF16), wc_ref[...]))
        xn_scr[...] = xn
        hn_scr[...] = _rms(xn, g_ref[...]).astype(BF16)
        acc_scr[...] = jnp.zeros_like(acc_scr)

    h = _dot(hn_scr[...], wu_ref[...])
    h = jnp.square(jnp.maximum(h, 0.0)).astype(BF16)
    acc_scr[...] += _dot(h, wd_ref[...])

    @pl.when(f == pl.num_programs(1) - 1)
    def _():
        out = xn_scr[...] + acc_scr[...]
        if final_norm:
            out = _rms(out, gf_ref[...])
        o_ref[...] = out


def _post(x, y, ca, wy, wc, g, wu, wd, gf, final_norm):
    m = x.shape[0]
    tm, tf = 1024, 512
    row = lambda i, f: (i, 0)
    const = lambda i, f: (0, 0)
    return pl.pallas_call(
        functools.partial(_post_kernel, final_norm=final_norm),
        grid=(m // tm, D_FF // tf),
        in_specs=[
            pl.BlockSpec((tm, D_MODEL), row),
            pl.BlockSpec((tm, y.shape[1]), row),
            pl.BlockSpec((tm, W_X), row),
            pl.BlockSpec(wy.shape, const),
            pl.BlockSpec(wc.shape, const),
            pl.BlockSpec((1, D_MODEL), const),
            pl.BlockSpec((D_MODEL, tf), lambda i, f: (0, f)),
            pl.BlockSpec((tf, D_MODEL), lambda i, f: (f, 0)),
            pl.BlockSpec((1, D_MODEL), const),
        ],
        out_specs=pl.BlockSpec((tm, D_MODEL), row),
        out_shape=jax.ShapeDtypeStruct((m, D_MODEL), F32),
        scratch_shapes=[
            pltpu.VMEM((tm, D_MODEL), F32),
            pltpu.VMEM((tm, D_MODEL), BF16),
            pltpu.VMEM((tm, D_MODEL), F32),
        ],
        compiler_params=_cparams("parallel", "arbitrary"),
        name="post",
    )(x, y, ca, wy, wc, g, wu, wd, gf)


def _gelu_tanh(x):
    return x * (0.5 * (1.0 + jnp.tanh(math.sqrt(2.0 / math.pi) * (x + 0.044715 * (x * x * x)))))


def _conv_rows(u, u_prev, cw_ref, cb_ref):
    rows = u.shape[0]
    g = rows // SUBLANES
    u3 = u.reshape(g, SUBLANES, W_A)
    p3 = u_prev.reshape(g, SUBLANES, W_A)
    sub = lax.broadcasted_iota(jnp.int32, (g, SUBLANES, W_A), 1)
    xc = cb_ref[...] + cw_ref[CONV_WIDTH - 1:CONV_WIDTH, :] * u3
    for j in range(1, CONV_WIDTH):
        sh = jnp.where(sub >= j, pltpu.roll(u3, j, axis=1), pltpu.roll(p3, j, axis=1))
        xc = xc + cw_ref[CONV_WIDTH - 1 - j:CONV_WIDTH - j, :] * sh
    return xc.reshape(rows, W_A)


def _lru_coeffs(xc, wcat_ref, gab_ref, gxb_ref, lam_ref, first_row, a_scr, b_scr):
    z = -lam_ref[...]
    softplus = jnp.maximum(z, 0.0) + jnp.log1p(jnp.exp(-jnp.abs(z)))
    for n in range(LRU_BLOCKS):
        sl = slice(n * LRU_BLOCK_DIM, (n + 1) * LRU_BLOCK_DIM)
        xn = xc[:, sl]
        ra = _dot(xn.astype(BF16), wcat_ref[n])
        r = jax.nn.sigmoid(ra[:, :LRU_BLOCK_DIM] + gab_ref[:, sl])
        ig = jax.nn.sigmoid(ra[:, LRU_BLOCK_DIM:] + gxb_ref[:, sl])
        log_a = -LRU_C * r * softplus[:, sl]
        th = jnp.tanh(log_a)
        mult = jnp.sqrt(-2.0 * th / (1.0 - th))
        if first_row is not None:
            mult = jnp.where(first_row, 1.0, mult)
        a_scr[:, sl] = jnp.exp(log_a)
        b_scr[:, sl] = mult * ig * xn


def _scan_within_tiles(a, b):
    rows = a.shape[0]
    g = rows // SUBLANES
    a3 = a.reshape(g, SUBLANES, W_A)
    b3 = b.reshape(g, SUBLANES, W_A)
    sub = lax.broadcasted_iota(jnp.int32, (g, SUBLANES, W_A), 1)
    for s in (1, 2, 4):
        keep = sub >= s
        b3 = jnp.where(keep, a3 * pltpu.roll(b3, s, axis=1) + b3, b3)
        a3 = jnp.where(keep, a3 * pltpu.roll(a3, s, axis=1), a3)
    return a3.reshape(rows, W_A), b3.reshape(rows, W_A)


def _lru_prompt_kernel(u_ref, gate_ref, wcat_ref, cw_ref, cb_ref, gab_ref, gxb_ref, lam_ref,
                       y_ref, utail_ref, htail_ref, tail_scr, carry_scr, a_scr, b_scr, h_scr, *, tile):
    t = pl.program_id(1)

    @pl.when(t == 0)
    def _():
        tail_scr[...] = jnp.zeros_like(tail_scr)
        carry_scr[...] = jnp.zeros_like(carry_scr)

    u = u_ref[...]
    u_prev = jnp.concatenate([tail_scr[...], u[:tile - SUBLANES]], axis=0)
    tail_scr[...] = u[tile - SUBLANES:]
    xc = _conv_rows(u, u_prev, cw_ref, cb_ref)
    row = lax.broadcasted_iota(jnp.int32, (tile, LRU_BLOCK_DIM), 0)
    first_row = jnp.logical_and(row == 0, t == 0)
    _lru_coeffs(xc, wcat_ref, gab_ref, gxb_ref, lam_ref, first_row, a_scr, b_scr)
    a, b = _scan_within_tiles(a_scr[...], b_scr[...])
    a_scr[...] = a
    b_scr[...] = b

    def step(g, carry):
        rows = pl.ds(pl.multiple_of(g * SUBLANES, SUBLANES), SUBLANES)
        h = b_scr[rows, :] + a_scr[rows, :] * carry
        h_scr[rows, :] = h
        return jnp.broadcast_to(h[SUBLANES - 1:SUBLANES, :], (SUBLANES, W_A))

    carry_scr[...] = lax.fori_loop(0, tile // SUBLANES, step, carry_scr[...])
    y_ref[...] = (_gelu_tanh(gate_ref[...]) * h_scr[...]).astype(y_ref.dtype)
    utail_ref[...] = u[tile - SUBLANES:]
    htail_ref[...] = h_scr[tile - SUBLANES:, :]


def _lru_prompt(u, gate, wcat, cw, cb, gab, gxb, lam):
    tile = 256
    nt = SEQ // tile
    row = lambda b, t: (b * nt + t, 0)
    const2 = lambda b, t: (0, 0)
    vec = pl.BlockSpec((1, W_A), const2)
    tail_spec = pl.BlockSpec((None, SUBLANES, W_A), lambda b, t: (b, 0, 0))
    return pl.pallas_call(
        functools.partial(_lru_prompt_kernel, tile=tile),
        grid=(BATCH, nt),
        in_specs=[
            pl.BlockSpec((tile, W_A), row),
            pl.BlockSpec((tile, W_A), row),
            pl.BlockSpec(wcat.shape, lambda b, t: (0, 0, 0)),
            pl.BlockSpec((CONV_WIDTH, W_A), const2),
            vec, vec, vec, vec,
        ],
        out_specs=[pl.BlockSpec((tile, W_A), row), tail_spec, tail_spec],
        out_shape=[
            jax.ShapeDtypeStruct((BATCH * SEQ, W_A), BF16),
            jax.ShapeDtypeStruct((BATCH, SUBLANES, W_A), F32),
            jax.ShapeDtypeStruct((BATCH, SUBLANES, W_A), F32),
        ],
        scratch_shapes=[
            pltpu.VMEM((SUBLANES, W_A), F32),
            pltpu.VMEM((SUBLANES, W_A), F32),
            pltpu.VMEM((tile, W_A), F32),
            pltpu.VMEM((tile, W_A), F32),
            pltpu.VMEM((tile, W_A), F32),
        ],
        compiler_params=_cparams("parallel", "arbitrary"),
        name="lru_prompt",
    )(u, gate, wcat, cw, cb, gab, gxb, lam)


def _lru_sample_kernel(u_ref, gate_ref, prev_ref, h0_ref, wcat_ref, cw_ref, cb_ref, gab_ref, gxb_ref, lam_ref,
                       y_ref, h_ref, a_scr, b_scr):
    xc = _conv_rows(u_ref[...], prev_ref[...], cw_ref, cb_ref)
    _lru_coeffs(xc, wcat_ref, gab_ref, gxb_ref, lam_ref, None, a_scr, b_scr)
    a = a_scr[...]
    _, h = _scan_within_tiles(a, b_scr[...] + a * h0_ref[...])
    h_ref[...] = h
    y_ref[...] = (_gelu_tanh(gate_ref[...]) * h).astype(y_ref.dtype)


def _lru_sample(u, gate, prev, h0pad, wcat, cw, cb, gab, gxb, lam):
    m = u.shape[0]
    tile = 256
    row = lambda i: (i, 0)
    const2 = lambda i: (0, 0)
    vec = pl.BlockSpec((1, W_A), const2)
    blk = pl.BlockSpec((tile, W_A), row)
    return pl.pallas_call(
        _lru_sample_kernel,
        grid=(m // tile,),
        in_specs=[blk, blk, blk, blk,
                  pl.BlockSpec(wcat.shape, lambda i: (0, 0, 0)),
                  pl.BlockSpec((CONV_WIDTH, W_A), const2),
                  vec, vec, vec, vec],
        out_specs=[blk, blk],
        out_shape=[jax.ShapeDtypeStruct((m, W_A), BF16), jax.ShapeDtypeStruct((m, W_A), F32)],
        scratch_shapes=[pltpu.VMEM((tile, W_A), F32), pltpu.VMEM((tile, W_A), F32)],
        compiler_params=_cparams("parallel"),
        name="lru_sample",
    )(u, gate, prev, h0pad, wcat, cw, cb, gab, gxb, lam)


def _bias_tile_kernel(rel_ref, o_ref):
    h = pl.program_id(0)
    w = pl.program_id(1)
    shape = (MOBA_BLOCK, MOBA_BLOCK)
    r = lax.broadcasted_iota(jnp.int32, shape, 0)
    c = lax.broadcasted_iota(jnp.int32, shape, 1)
    dist = jnp.maximum(w * MOBA_BLOCK + c - r, 0)
    large = jnp.full(shape, REL_BUCKETS // 2, jnp.int32)
    for thr in T5_THRESHOLDS:
        large = large + (dist >= thr).astype(jnp.int32)
    bucket = jnp.where(dist < REL_BUCKETS // 2, dist, large)
    out = jnp.zeros(shape, F32)
    for b in range(REL_BUCKETS):
        out = jnp.where(bucket == b, rel_ref[b, h], out)
    o_ref[...] = out


def _bias_tiles(rel_bias):
    return pl.pallas_call(
        _bias_tile_kernel,
        grid=(B_HEADS, 2),
        in_specs=[pl.BlockSpec(memory_space=pltpu.SMEM)],
        out_specs=pl.BlockSpec((None, None, MOBA_BLOCK, MOBA_BLOCK), lambda h, w: (h, w, 0, 0)),
        out_shape=jax.ShapeDtypeStruct((B_HEADS, 2, MOBA_BLOCK, MOBA_BLOCK), F32),
        compiler_params=_cparams("parallel", "parallel"),
        name="bias_tiles",
    )(rel_bias)


def _select_topk(gate, n_past):
    blk = lax.broadcasted_iota(jnp.int32, gate.shape, 1)
    past = blk < n_past
    g = jnp.where(past, gate, NEG_INF)
    rank = jnp.zeros(gate.shape, jnp.int32)
    for m in range(gate.shape[1]):
        gm = g[:, m:m + 1]
        beats = jnp.logical_or(gm > g, jnp.logical_and(gm == g, blk > m))
        rank = rank + beats.astype(jnp.int32)
    return jnp.where(jnp.logical_and(past, rank < MOBA_TOPK), 1.0, 0.0)


def _select_topk_t(gate_t, n_past):
    blk = lax.broadcasted_iota(jnp.int32, gate_t.shape, 0)
    past = blk < n_past
    g = jnp.where(past, gate_t, NEG_INF)
    rank = jnp.zeros(gate_t.shape, jnp.int32)
    for m in range(gate_t.shape[0]):
        gm = g[m:m + 1, :]
        beats = jnp.logical_or(gm > g, jnp.logical_and(gm == g, blk > m))
        rank = rank + beats.astype(jnp.int32)
    return jnp.where(jnp.logical_and(past, rank < MOBA_TOPK), 1.0, 0.0)


def _moba_prompt_kernel(rel_ref, q_ref, k_ref, v_ref, kmean_ref, bias_ref, o_ref, vt_scr, s_scr):
    h = pl.program_id(1)
    scale = B_HEAD_DIM ** -0.5

    for n in range(N_BLOCKS):
        cols = slice(n * MOBA_BLOCK, (n + 1) * MOBA_BLOCK)
        vt_scr[:, cols] = v_ref[cols, :].astype(F32).T.astype(BF16)

    gate_all = _dot_nt(kmean_ref[...], q_ref[...], precision=lax.Precision.HIGHEST)
    query_block = lax.broadcasted_iota(jnp.int32, gate_all.shape, 1) // MOBA_BLOCK
    penalty_all = jnp.where(_select_topk_t(gate_all, query_block) > 0.0, 0.0, NEG_INF)
    far_bias = rel_ref[REL_BUCKETS - 1, h]
    key = lax.broadcasted_iota(jnp.int32, (MOBA_BLOCK, MOBA_BLOCK), 0)
    query = lax.broadcasted_iota(jnp.int32, (MOBA_BLOCK, MOBA_BLOCK), 1)

    def tile_rows(x, op):
        return op(x.reshape(MOBA_BLOCK // SUBLANES, SUBLANES, MOBA_BLOCK), axis=0)

    def attend(own):
        q_rows = slice(own * MOBA_BLOCK, (own + 1) * MOBA_BLOCK)
        qb = q_ref[q_rows, :].astype(BF16)
        penalty_t = penalty_all[:, q_rows]
        first_tile = own * (own + 1) // 2
        m_part = None
        for n in range(own, -1, -1):
            rows = slice(n * MOBA_BLOCK, (n + 1) * MOBA_BLOCK)
            s = _dot_nt(k_ref[rows, :], qb) * scale
            if n == own:
                s = jnp.where(key <= query, s + bias_ref[0], NEG_INF)
            elif n == own - 1:
                s = s + bias_ref[1] + penalty_t[n:n + 1, :]
            else:
                s = s + (far_bias + penalty_t[n:n + 1, :])
            s_scr[first_tile + n] = s
            part = tile_rows(s, jnp.max)
            m_part = part if m_part is None else jnp.maximum(m_part, part)
        m = jnp.max(m_part, axis=0, keepdims=True)
        l_part = jnp.zeros((SUBLANES, MOBA_BLOCK), F32)
        acc = jnp.zeros((B_HEAD_DIM, MOBA_BLOCK), F32)
        for n in range(own + 1):
            rows = slice(n * MOBA_BLOCK, (n + 1) * MOBA_BLOCK)
            p = jnp.exp(s_scr[first_tile + n] - m)
            l_part = l_part + tile_rows(p, jnp.sum)
            acc = acc + _dot(vt_scr[:, rows], p.astype(BF16))
        l = jnp.sum(l_part, axis=0, keepdims=True)
        o_ref[q_rows, :] = (acc / l).T.astype(o_ref.dtype)

    for own in range(N_BLOCKS):
        attend(own)


def _moba_prompt(q, k_bf, v_bf, kmean, bias_tiles, rel_bias):
    batch = q.shape[0] // SEQ
    seq_spec = pl.BlockSpec((SEQ, B_HEAD_DIM), lambda b, h: (b, h))
    n_tiles = N_BLOCKS * (N_BLOCKS + 1) // 2
    return pl.pallas_call(
        _moba_prompt_kernel,
        grid=(batch, B_HEADS),
        in_specs=[
            pl.BlockSpec(memory_space=pltpu.SMEM),
            seq_spec, seq_spec, seq_spec,
            pl.BlockSpec((None, N_BLOCKS, B_HEAD_DIM), lambda b, h: (b, 0, h)),
            pl.BlockSpec((None, 2, MOBA_BLOCK, MOBA_BLOCK), lambda b, h: (h, 0, 0, 0)),
        ],
        out_specs=seq_spec,
        out_shape=jax.ShapeDtypeStruct((batch * SEQ, W_B), BF16),
        scratch_shapes=[pltpu.VMEM((B_HEAD_DIM, SEQ), BF16),
                        pltpu.VMEM((n_tiles, MOBA_BLOCK, MOBA_BLOCK), F32)],
        compiler_params=_cparams("parallel", "parallel"),
        name="moba_prompt",
    )(rel_bias, q, k_bf, v_bf, kmean, bias_tiles)


SAMPLE_ROWS = DEC_SEQ * B_HEADS
PAGE_ROWS = PAGE_SIZE * B_HEADS


def _moba_sample_kernel(pt_ref, q_ref, kn_ref, vn_ref, bprev_ref, bnew_ref, far_ref, *refs):
    del pt_ref
    k_pages = refs[:N_PAGES]
    v_pages = refs[N_PAGES:2 * N_PAGES]
    o_ref = refs[2 * N_PAGES]
    s_scr = refs[2 * N_PAGES + 1]
    scale = B_HEAD_DIM ** -0.5
    pages_per_block = MOBA_BLOCK // PAGE_SIZE
    n_past = PAST_LEN // MOBA_BLOCK

    q = q_ref[...]
    qb = q.astype(BF16)
    q3 = q.reshape(DEC_SEQ, B_HEADS, B_HEAD_DIM)
    gate_cols = []
    for n in range(n_past):
        ksum = sum(k_pages[n * pages_per_block + j][...].reshape(PAGE_SIZE, B_HEADS, B_HEAD_DIM).sum(axis=0)
                   for j in range(pages_per_block))
        kmean = ksum * (1.0 / MOBA_BLOCK)
        gate_cols.append(jnp.sum(q3 * kmean[None], axis=-1, keepdims=True).reshape(SAMPLE_ROWS, 1))
    gate = jnp.concatenate(gate_cols, axis=1)
    penalty = jnp.where(_select_topk(gate, n_past) > 0.0, 0.0, NEG_INF)

    def same_head_penalty(cols):
        r = lax.broadcasted_iota(jnp.int32, (SAMPLE_ROWS, cols), 0) % B_HEADS
        c = lax.broadcasted_iota(jnp.int32, (SAMPLE_ROWS, cols), 1) % B_HEADS
        return jnp.where(r == c, 0.0, NEG_INF)

    head_pen = same_head_penalty(PAGE_ROWS)
    far_bias = far_ref[...]
    m_el = jnp.full((SAMPLE_ROWS, PAGE_ROWS), NEG_INF, F32)
    for p in range(N_PAGES):
        n = p // pages_per_block
        s = _dot_nt(qb, k_pages[p][...].astype(BF16)) * scale
        if n == n_past - 1:
            j = p - n * pages_per_block
            s = s + bprev_ref[:, j * PAGE_ROWS:(j + 1) * PAGE_ROWS] + penalty[:, n:n + 1]
        else:
            s = s + (far_bias + penalty[:, n:n + 1])
        s = s + head_pen
        s_scr[p] = s
        m_el = jnp.maximum(m_el, s)

    s_new = _dot_nt(q, kn_ref[...]) * scale + bnew_ref[...] + same_head_penalty(SAMPLE_ROWS)
    t_q = lax.broadcasted_iota(jnp.int32, (SAMPLE_ROWS, SAMPLE_ROWS), 0) // B_HEADS
    t_k = lax.broadcasted_iota(jnp.int32, (SAMPLE_ROWS, SAMPLE_ROWS), 1) // B_HEADS
    s_new = jnp.where(t_k <= t_q, s_new, NEG_INF)

    m = jnp.maximum(jnp.max(m_el, axis=-1, keepdims=True), jnp.max(s_new, axis=-1, keepdims=True))
    p_new = jnp.exp(s_new - m)
    acc = _dot(p_new, vn_ref[...])
    l_el = jnp.zeros((SAMPLE_ROWS, PAGE_ROWS), F32)
    for p in range(N_PAGES):
        pr = jnp.exp(s_scr[p] - m)
        l_el = l_el + pr
        acc = acc + _dot(pr.astype(BF16), v_pages[p][...].astype(BF16))
    l = jnp.sum(p_new, axis=-1, keepdims=True) + jnp.sum(l_el, axis=-1, keepdims=True)
    o_ref[...] = acc / l


def _moba_sample(q_rows, k_rows, v_rows, cache_k_rows, cache_v_rows, layer, page_table_flat,
                 bias_prev, bias_new, far_col):
    n_seq = q_rows.shape[0] // SAMPLE_ROWS
    tok_spec = pl.BlockSpec((SAMPLE_ROWS, B_HEAD_DIM), lambda b, pt: (b, 0))

    def page_spec(p):
        return pl.BlockSpec((None, None, PAGE_ROWS, B_HEAD_DIM),
                            lambda b, pt: (layer, pt[b * N_PAGES + p], 0, 0))

    def const_spec(a):
        return pl.BlockSpec(a.shape, lambda b, pt: (0, 0))

    pages = [page_spec(p) for p in range(N_PAGES)]
    grid_spec = pltpu.PrefetchScalarGridSpec(
        num_scalar_prefetch=1,
        grid=(n_seq,),
        in_specs=[tok_spec, tok_spec, tok_spec, const_spec(bias_prev), const_spec(bias_new),
                  const_spec(far_col)] + pages + pages,
        out_specs=tok_spec,
        scratch_shapes=[pltpu.VMEM((N_PAGES, SAMPLE_ROWS, PAGE_ROWS), F32)],
    )
    return pl.pallas_call(
        _moba_sample_kernel,
        grid_spec=grid_spec,
        out_shape=jax.ShapeDtypeStruct((n_seq * SAMPLE_ROWS, B_HEAD_DIM), F32),
        compiler_params=_cparams("parallel"),
        name="moba_sample",
    )(page_table_flat, q_rows, k_rows, v_rows, bias_prev, bias_new, far_col,
      *([cache_k_rows] * N_PAGES), *([cache_v_rows] * N_PAGES))


def kernel(x_prompt, x_sample, cache_k, cache_v, state_conv, state_h, cache_mem_k, cache_mem_v, page_table,
           mem_prompt, norm_mix, norm_mlp, norm_mem, norm_final, w_in_a, w_out_a, conv_w, conv_b, gate_a_w,
           gate_a_b, gate_x_w, gate_x_b, lru_lambda, w_in_b, w_out_b, rel_bias, w_mem_kv, w_up, w_down):
    n_prompt = BATCH * SEQ
    n_sample = DEC_BATCH * DEC_SEQ
    xp = x_prompt.reshape(n_prompt, D_MODEL)
    xs = x_sample.reshape(n_sample, D_MODEL)

    w_in_a_bf, w_in_b_bf = w_in_a.astype(BF16), w_in_b.astype(BF16)
    w_out_a_bf, w_out_b_bf = w_out_a.astype(BF16), w_out_b.astype(BF16)
    w_up_bf, w_down_bf = w_up.astype(BF16), w_down.astype(BF16)
    wcat_bf = jnp.concatenate([gate_a_w, gate_x_w], axis=-1).astype(BF16)
    gf = norm_final.reshape(1, D_MODEL)

    memk, memv = _mem_kv(mem_prompt.reshape(BATCH * MEM_LEN, D_MODEL), norm_mem.reshape(DEPTH, 1, D_MODEL),
                         w_mem_kv.astype(BF16))
    cmk = cache_mem_k.reshape(DEPTH, DEC_BATCH, XMEM_ROWS, X_HEAD_DIM)
    cmv = cache_mem_v.reshape(DEPTH, DEC_BATCH, XMEM_ROWS, X_HEAD_DIM)
    cache_k_rows = cache_k.reshape(cache_k.shape[0], cache_k.shape[1], PAGE_ROWS, B_HEAD_DIM)
    cache_v_rows = cache_v.reshape(cache_v.shape[0], cache_v.shape[1], PAGE_ROWS, B_HEAD_DIM)
    pt_flat = page_table.reshape(DEC_BATCH * N_PAGES)

    tiles = _bias_tiles(rel_bias)
    bias_prev = jnp.transpose(tiles[:, 1, :, :DEC_SEQ], (2, 1, 0)).reshape(DEC_SEQ, 1, MOBA_BLOCK * B_HEADS)
    bias_prev = jnp.broadcast_to(bias_prev, (DEC_SEQ, B_HEADS, MOBA_BLOCK * B_HEADS))
    bias_prev = bias_prev.reshape(SAMPLE_ROWS, MOBA_BLOCK * B_HEADS)
    bias_new = jnp.transpose(tiles[:, 0, :DEC_SEQ, :DEC_SEQ], (2, 1, 0)).reshape(DEC_SEQ, 1, SAMPLE_ROWS)
    bias_new = jnp.broadcast_to(bias_new, (DEC_SEQ, B_HEADS, SAMPLE_ROWS)).reshape(SAMPLE_ROWS, SAMPLE_ROWS)
    far_col = jnp.tile(rel_bias[REL_BUCKETS - 1], DEC_SEQ).reshape(SAMPLE_ROWS, 1)

    kp_l, vp_l, ks_l, vs_l = [], [], [], []
    cp_l, hp_l, cs_l, hs_l = [], [], [], []
    for i in range(DEPTH):
        j = i // 2
        g_mix = norm_mix[i].reshape(1, D_MODEL)
        g_mlp = norm_mlp[i].reshape(1, D_MODEL)
        final = i == DEPTH - 1
        if i % 2 == 0:
            lru_w = (wcat_bf[j], conv_w[j], conv_b[j].reshape(1, W_A), gate_a_b[j].reshape(1, W_A),
                     gate_x_b[j].reshape(1, W_A), lru_lambda[j].reshape(1, W_A))
            w_out = w_out_a_bf[j]
            mix_w = W_A

            def outs_a(qx_kind, qx_dtype):
                return ((0, W_A, "tok", F32), (W_A, W_A, "tok", F32), (2 * W_A, W_X, qx_kind, qx_dtype))

            u, gate, qx = _norm_proj(xp, g_mix, w_in_a_bf[j], outs_a("tok", BF16))
            y_p, utail, htail = _lru_prompt(u, gate, *lru_w)
            cp_l.append(utail[:, SUBLANES - (CONV_WIDTH - 1):, :])
            hp_l.append(htail[:, SUBLANES - 1, :])

            u, gate, qx_s = _norm_proj(xs, g_mix, w_in_a_bf[j], outs_a("rows", F32))
            prev = jnp.pad(state_conv[j], ((0, 0), (SUBLANES - (CONV_WIDTH - 1), 0), (0, 0)))
            h0pad = jnp.pad(state_h[j][:, None, :], ((0, 0), (0, DEC_SEQ - 1), (0, 0)))
            y_s, h_s = _lru_sample(u, gate, prev.reshape(n_sample, W_A), h0pad.reshape(n_sample, W_A), *lru_w)
            cs_l.append(u.reshape(DEC_BATCH, DEC_SEQ, W_A)[:, DEC_SEQ - (CONV_WIDTH - 1):, :])
            hs_l.append(h_s.reshape(DEC_BATCH, DEC_SEQ, W_A)[:, DEC_SEQ - 1, :])
        else:
            w_out = w_out_b_bf[j]
            mix_w = W_B

            outs_p = ((0, W_B, "tok", F32),
                      (W_B, W_B, "rows", F32), (W_B, W_B, "tok", BF16), (W_B, W_B, "blockmean", F32),
                      (2 * W_B, W_B, "rows", F32), (2 * W_B, W_B, "tok", BF16),
                      (3 * W_B, W_X, "tok", BF16))
            q, k_rows, k_bf, kmean, v_rows, v_bf, qx = _norm_proj(xp, g_mix, w_in_b_bf[j], outs_p)
            y_p = _moba_prompt(q, k_bf, v_bf, kmean.reshape(BATCH, N_BLOCKS, W_B), tiles, rel_bias)
            kp_l.append(k_rows.reshape(BATCH, SEQ, B_HEADS, B_HEAD_DIM))
            vp_l.append(v_rows.reshape(BATCH, SEQ, B_HEADS, B_HEAD_DIM))

            outs_s = ((0, W_B, "rows", F32), (W_B, W_B, "rows", F32), (2 * W_B, W_B, "rows", F32),
                      (3 * W_B, W_X, "rows", F32))
            q_rows, k_rows, v_rows, qx_s = _norm_proj(xs, g_mix, w_in_b_bf[j], outs_s)
            y_s = _moba_sample(q_rows, k_rows, v_rows, cache_k_rows, cache_v_rows, j, pt_flat,
                               bias_prev, bias_new, far_col).reshape(n_sample, W_B)
            ks_l.append(k_rows.reshape(DEC_BATCH, DEC_SEQ, B_HEADS, B_HEAD_DIM))
            vs_l.append(v_rows.reshape(DEC_BATCH, DEC_SEQ, B_HEADS, B_HEAD_DIM))

        ca_p = _xattn_prompt(qx, memk, memv, i)
        ca_s = _xattn_sample(qx_s, cmk, cmv, i).reshape(n_sample, W_X)
        post_w = (w_out[:mix_w], w_out[mix_w:], g_mlp, w_up_bf[i], w_down_bf[i], gf, final)
        xp = _post(xp, y_p, ca_p, *post_w)
        xs = _post(xs, y_s, ca_s, *post_w)

    mem_shape = (DEPTH, BATCH, MEM_LEN, X_HEADS, X_HEAD_DIM)
    return (xp.reshape(BATCH, SEQ, D_MODEL), xs.reshape(DEC_BATCH, DEC_SEQ, D_MODEL),
            jnp.stack(kp_l), jnp.stack(vp_l), jnp.stack(ks_l), jnp.stack(vs_l),
            jnp.stack(cp_l), jnp.stack(hp_l), jnp.stack(cs_l), jnp.stack(hs_l),
            memk.reshape(mem_shape), memv.reshape(mem_shape))
```

```python
import functools
import math

import jax
import jax.numpy as jnp
from jax import lax
from jax.experimental import pallas as pl
from jax.experimental.pallas import tpu as pltpu

F32 = jnp.float32
BF16 = jnp.bfloat16

D_MODEL = 1024
BATCH = 8
SEQ = 2048
DEPTH = 4
DEC_BATCH = 128
DEC_SEQ = 8
PAST_LEN = 2048
PAGE_SIZE = 128
N_PAGES = PAST_LEN // PAGE_SIZE
W_A = D_MODEL
LRU_BLOCKS = 8
LRU_BLOCK_DIM = W_A // LRU_BLOCKS
CONV_WIDTH = 4
LRU_C = 8.0
B_HEADS = 8
B_HEAD_DIM = D_MODEL // B_HEADS
W_B = B_HEADS * B_HEAD_DIM
MOBA_BLOCK = 256
MOBA_TOPK = 3
N_BLOCKS = SEQ // MOBA_BLOCK
REL_BUCKETS = 32
REL_MAX_DIST = 128
X_HEADS = 4
X_HEAD_DIM = 128
W_X = X_HEADS * X_HEAD_DIM
MEM_LEN = 256
D_FF = 4 * D_MODEL
EPS = 1e-6

SUBLANES = 8
VMEM_LIMIT = 56 * 1024 * 1024
NEG_INF = float("-inf")


def _cparams(*sem):
    return pltpu.CompilerParams(dimension_semantics=sem, vmem_limit_bytes=VMEM_LIMIT)


def _dot(a, b):
    return jnp.dot(a, b, preferred_element_type=F32)


def _dot_nt(a, b, precision=None):
    return lax.dot_general(a, b, (((1,), (1,)), ((), ())), preferred_element_type=F32, precision=precision)


def _rms(x, g):
    return x * lax.rsqrt(jnp.mean(x * x, axis=-1, keepdims=True) + EPS) * g


def _t5_thresholds():
    max_exact = REL_BUCKETS // 2
    thr = []
    for k in range(1, REL_BUCKETS - max_exact):
        d = max_exact
        while int(math.log(d / max_exact) / math.log(REL_MAX_DIST / max_exact) * (REL_BUCKETS - max_exact)) < k:
            d += 1
        thr.append(d)
    return thr


T5_THRESHOLDS = _t5_thresholds()


def _mem_kv_kernel(x_ref, g_ref, w_ref, k_ref, v_ref):
    hn = _rms(x_ref[...], g_ref[...]).astype(BF16)
    r = _dot(hn, w_ref[...])
    k_ref[...] = r[:, :W_X]
    v_ref[...] = r[:, W_X:]


def _mem_kv(mem2d, norm_mem, w_bf):
    m = mem2d.shape[0]
    tm = 512
    return pl.pallas_call(
        _mem_kv_kernel,
        grid=(DEPTH, m // tm),
        in_specs=[
            pl.BlockSpec((tm, D_MODEL), lambda l, i: (i, 0)),
            pl.BlockSpec((None, 1, D_MODEL), lambda l, i: (l, 0, 0)),
            pl.BlockSpec((None, D_MODEL, 2 * W_X), lambda l, i: (l, 0, 0)),
        ],
        out_specs=[pl.BlockSpec((None, tm, W_X), lambda l, i: (l, i, 0))] * 2,
        out_shape=[jax.ShapeDtypeStruct((DEPTH, m, W_X), F32)] * 2,
        compiler_params=_cparams("parallel", "parallel"),
        name="mem_kv",
    )(mem2d, norm_mem, w_bf)


LANES = 128


def _norm_proj_kernel(x_ref, g_ref, w_ref, *out_refs, outs):
    tm = x_ref.shape[0]
    for c in range(tm // MOBA_BLOCK):
        rows = slice(c * MOBA_BLOCK, (c + 1) * MOBA_BLOCK)
        hn = _rms(x_ref[rows, :], g_ref[...]).astype(BF16)
        results = {}
        for o_ref, (off, n, kind, _) in zip(out_refs, outs):
            if off not in results:
                results[off] = _dot(hn, w_ref[:, off:off + n])
            r = results[off]
            if kind == "tok":
                o_ref[rows, :] = r.astype(o_ref.dtype)
            elif kind == "rows":
                heads = n // LANES
                for h in range(heads):
                    o_ref[pl.ds(c * MOBA_BLOCK * heads + h, MOBA_BLOCK, stride=heads), :] = (
                        r[:, h * LANES:(h + 1) * LANES])
            else:
                o_ref[c] = jnp.mean(r, axis=0, keepdims=True)


def _norm_proj(x, g, w_bf, mixer_layer, outs):
    m = x.shape[0]
    tm = 512
    n_all = w_bf.shape[2]
    specs, shapes = [], []
    for _, n, kind, dt in outs:
        if kind == "tok":
            specs.append(pl.BlockSpec((tm, n), lambda i: (i, 0)))
            shapes.append(jax.ShapeDtypeStruct((m, n), dt))
        elif kind == "rows":
            heads = n // LANES
            specs.append(pl.BlockSpec((tm * heads, LANES), lambda i: (i, 0)))
            shapes.append(jax.ShapeDtypeStruct((m * heads, LANES), dt))
        else:
            specs.append(pl.BlockSpec((tm // MOBA_BLOCK, 1, n), lambda i: (i, 0, 0)))
            shapes.append(jax.ShapeDtypeStruct((m // MOBA_BLOCK, 1, n), dt))
    return pl.pallas_call(
        functools.partial(_norm_proj_kernel, outs=outs),
        grid=(m // tm,),
        in_specs=[
            pl.BlockSpec((tm, D_MODEL), lambda i: (i, 0)),
            pl.BlockSpec((1, D_MODEL), lambda i: (0, 0)),
            pl.BlockSpec((None, D_MODEL, n_all), lambda i: (mixer_layer, 0, 0)),
        ],
        out_specs=specs,
        out_shape=shapes,
        compiler_params=_cparams("parallel"),
        name="norm_proj",
    )(x, g, w_bf)


def _softmax_pv(s, v_bf):
    m = jnp.max(s, axis=-1, keepdims=True)
    p = jnp.exp(s - m)
    l = jnp.sum(p, axis=-1, keepdims=True)
    return _dot(p.astype(BF16), v_bf) / l


def _xattn_prompt_kernel(q_ref, k_ref, v_ref, o_ref):
    scale = X_HEAD_DIM ** -0.5
    for h in range(X_HEADS):
        sl = slice(h * X_HEAD_DIM, (h + 1) * X_HEAD_DIM)
        s = _dot_nt(q_ref[:, sl], k_ref[:, sl].astype(BF16)) * scale
        o_ref[:, sl] = _softmax_pv(s, v_ref[:, sl].astype(BF16)).astype(o_ref.dtype)


def _xattn_prompt(qx, memk, memv, layer):
    tq = 512
    nt = SEQ // tq
    kv_spec = pl.BlockSpec((None, MEM_LEN, W_X), lambda b, t: (layer, b, 0))
    return pl.pallas_call(
        _xattn_prompt_kernel,
        grid=(BATCH, nt),
        in_specs=[pl.BlockSpec((tq, W_X), lambda b, t: (b * nt + t, 0)), kv_spec, kv_spec],
        out_specs=pl.BlockSpec((tq, W_X), lambda b, t: (b * nt + t, 0)),
        out_shape=jax.ShapeDtypeStruct((BATCH * SEQ, W_X), BF16),
        compiler_params=_cparams("parallel", "parallel"),
        name="xattn_prompt",
    )(qx, memk, memv)


XQ_ROWS = DEC_SEQ * X_HEADS
XMEM_ROWS = MEM_LEN * X_HEADS


def _xattn_sample_kernel(q_ref, k_ref, v_ref, o_ref, *, group):
    scale = X_HEAD_DIM ** -0.5
    r = lax.broadcasted_iota(jnp.int32, (XQ_ROWS, XMEM_ROWS), 0) % X_HEADS
    c = lax.broadcasted_iota(jnp.int32, (XQ_ROWS, XMEM_ROWS), 1) % X_HEADS
    head_pen = jnp.where(r == c, 0.0, NEG_INF)
    for g in range(group):
        rows = slice(g * XQ_ROWS, (g + 1) * XQ_ROWS)
        s = _dot_nt(q_ref[rows, :].astype(BF16), k_ref[g].astype(BF16)) * scale + head_pen
        o_ref[rows, :] = _softmax_pv(s, v_ref[g].astype(BF16))


def _xattn_sample(q_rows, cmk_rows, cmv_rows, layer):
    group = 8
    n_seq = q_rows.shape[0] // XQ_ROWS
    kv_spec = pl.BlockSpec((None, group, XMEM_ROWS, X_HEAD_DIM), lambda i: (layer, i, 0, 0))
    q_spec = pl.BlockSpec((group * XQ_ROWS, X_HEAD_DIM), lambda i: (i, 0))
    return pl.pallas_call(
        functools.partial(_xattn_sample_kernel, group=group),
        grid=(n_seq // group,),
        in_specs=[q_spec, kv_spec, kv_spec],
        out_specs=q_spec,
        out_shape=jax.ShapeDtypeStruct((n_seq * XQ_ROWS, X_HEAD_DIM), F32),
        compiler_params=_cparams("parallel"),
        name="xattn_sample",
    )(q_rows, cmk_rows, cmv_rows)


POST_ROW_CHUNK = 256


def _post_kernel(x_ref, y_ref, c_ref, wy_ref, wc_ref, g_ref, wu_ref, wd_ref, gf_ref, o_ref, hn_scr, *, final_norm):
    f = pl.program_id(1)
    tm = o_ref.shape[0]

    @pl.when(f == 0)
    def _():
        for c in range(tm // POST_ROW_CHUNK):
            rows = slice(c * POST_ROW_CHUNK, (c + 1) * POST_ROW_CHUNK)
            xn = (x_ref[rows, :] + _dot(y_ref[rows, :].astype(BF16), wy_ref[...])
                  + _dot(c_ref[rows, :].astype(BF16), wc_ref[...]))
            o_ref[rows, :] = xn
            hn_scr[rows, :] = _rms(xn, g_ref[...]).astype(BF16)

    h = _dot(hn_scr[...], wu_ref[...])
    h = jnp.square(jnp.maximum(h, 0.0)).astype(BF16)
    o_ref[...] += _dot(h, wd_ref[...])

    if final_norm:
        @pl.when(f == pl.num_programs(1) - 1)
        def _():
            o_ref[...] = _rms(o_ref[...], gf_ref[...])


def _post(x, y, ca, w_out, mixer_layer, g, w_up, w_down, layer, gf, final_norm):
    m = x.shape[0]
    tm, tf = 1024, 1024
    mix_w = y.shape[1]
    row = lambda i, f: (i, 0)
    const = lambda i, f: (0, 0)
    return pl.pallas_call(
        functools.partial(_post_kernel, final_norm=final_norm),
        grid=(m // tm, D_FF // tf),
        in_specs=[
            pl.BlockSpec((tm, D_MODEL), row),
            pl.BlockSpec((tm, mix_w), row),
            pl.BlockSpec((tm, W_X), row),
            pl.BlockSpec((None, mix_w, D_MODEL), lambda i, f: (mixer_layer, 0, 0)),
            pl.BlockSpec((None, W_X, D_MODEL), lambda i, f: (mixer_layer, mix_w // W_X, 0)),
            pl.BlockSpec((1, D_MODEL), const),
            pl.BlockSpec((None, D_MODEL, tf), lambda i, f: (layer, 0, f)),
            pl.BlockSpec((None, tf, D_MODEL), lambda i, f: (layer, f, 0)),
            pl.BlockSpec((1, D_MODEL), const),
        ],
        out_specs=pl.BlockSpec((tm, D_MODEL), row),
        out_shape=jax.ShapeDtypeStruct((m, D_MODEL), F32),
        scratch_shapes=[pltpu.VMEM((tm, D_MODEL), BF16)],
        compiler_params=_cparams("parallel", "arbitrary"),
        name="post",
    )(x, y, ca, w_out, w_out, g, w_up, w_down, gf)


def _gelu_tanh(x):
    return x * (0.5 * (1.0 + jnp.tanh(math.sqrt(2.0 / math.pi) * (x + 0.044715 * (x * x * x)))))


def _conv_rows(u, u_prev, cw_ref, cb_ref):
    rows = u.shape[0]
    g = rows // SUBLANES
    u3 = u.reshape(g, SUBLANES, W_A)
    p3 = u_prev.reshape(g, SUBLANES, W_A)
    sub = lax.broadcasted_iota(jnp.int32, (g, SUBLANES, W_A), 1)
    xc = cb_ref[...] + cw_ref[CONV_WIDTH - 1:CONV_WIDTH, :] * u3
    for j in range(1, CONV_WIDTH):
        sh = jnp.where(sub >= j, pltpu.roll(u3, j, axis=1), pltpu.roll(p3, j, axis=1))
        xc = xc + cw_ref[CONV_WIDTH - 1 - j:CONV_WIDTH - j, :] * sh
    return xc.reshape(rows, W_A)


def _lru_coeffs(xc, wcat_ref, gab_ref, gxb_ref, lam_ref, first_row, a_scr, b_scr):
    z = -lam_ref[...]
    softplus = jnp.maximum(z, 0.0) + jnp.log1p(jnp.exp(-jnp.abs(z)))
    for n in range(LRU_BLOCKS):
        sl = slice(n * LRU_BLOCK_DIM, (n + 1) * LRU_BLOCK_DIM)
        xn = xc[:, sl]
        ra = _dot(xn.astype(BF16), wcat_ref[n])
        r = jax.nn.sigmoid(ra[:, :LRU_BLOCK_DIM] + gab_ref[:, sl])
        ig = jax.nn.sigmoid(ra[:, LRU_BLOCK_DIM:] + gxb_ref[:, sl])
        log_a = -LRU_C * r * softplus[:, sl]
        th = jnp.tanh(log_a)
        mult = jnp.sqrt(-2.0 * th / (1.0 - th))
        if first_row is not None:
            mult = jnp.where(first_row, 1.0, mult)
        a_scr[:, sl] = jnp.exp(log_a)
        b_scr[:, sl] = mult * ig * xn


def _scan_within_tiles(a, b):
    rows = a.shape[0]
    g = rows // SUBLANES
    a3 = a.reshape(g, SUBLANES, W_A)
    b3 = b.reshape(g, SUBLANES, W_A)
    sub = lax.broadcasted_iota(jnp.int32, (g, SUBLANES, W_A), 1)
    for s in (1, 2, 4):
        keep = sub >= s
        b3 = jnp.where(keep, a3 * pltpu.roll(b3, s, axis=1) + b3, b3)
        a3 = jnp.where(keep, a3 * pltpu.roll(a3, s, axis=1), a3)
    return a3.reshape(rows, W_A), b3.reshape(rows, W_A)


def _lru_prompt_kernel(u_ref, gate_ref, wcat_ref, cw_ref, cb_ref, gab_ref, gxb_ref, lam_ref,
                       y_ref, utail_ref, htail_ref, tail_scr, carry_scr, a_scr, b_scr, h_scr, *, tile):
    t = pl.program_id(1)

    @pl.when(t == 0)
    def _():
        tail_scr[...] = jnp.zeros_like(tail_scr)
        carry_scr[...] = jnp.zeros_like(carry_scr)

    u = u_ref[...]
    u_prev = jnp.concatenate([tail_scr[...], u[:tile - SUBLANES]], axis=0)
    tail_scr[...] = u[tile - SUBLANES:]
    xc = _conv_rows(u, u_prev, cw_ref, cb_ref)
    row = lax.broadcasted_iota(jnp.int32, (tile, LRU_BLOCK_DIM), 0)
    first_row = jnp.logical_and(row == 0, t == 0)
    _lru_coeffs(xc, wcat_ref, gab_ref, gxb_ref, lam_ref, first_row, a_scr, b_scr)
    a, b = _scan_within_tiles(a_scr[...], b_scr[...])
    a_scr[...] = a
    b_scr[...] = b

    def step(g, carry):
        rows = pl.ds(pl.multiple_of(g * SUBLANES, SUBLANES), SUBLANES)
        h = b_scr[rows, :] + a_scr[rows, :] * carry
        h_scr[rows, :] = h
        return jnp.broadcast_to(h[SUBLANES - 1:SUBLANES, :], (SUBLANES, W_A))

    carry_scr[...] = lax.fori_loop(0, tile // SUBLANES, step, carry_scr[...])
    y_ref[...] = (_gelu_tanh(gate_ref[...]) * h_scr[...]).astype(y_ref.dtype)
    utail_ref[...] = u[tile - SUBLANES:]
    htail_ref[...] = h_scr[tile - SUBLANES:, :]


def _lru_prompt(u, gate, wcat, cw, cb, gab, gxb, lam):
    tile = 256
    nt = SEQ // tile
    row = lambda b, t: (b * nt + t, 0)
    const2 = lambda b, t: (0, 0)
    vec = pl.BlockSpec((1, W_A), const2)
    tail_spec = pl.BlockSpec((None, SUBLANES, W_A), lambda b, t: (b, 0, 0))
    return pl.pallas_call(
        functools.partial(_lru_prompt_kernel, tile=tile),
        grid=(BATCH, nt),
        in_specs=[
            pl.BlockSpec((tile, W_A), row),
            pl.BlockSpec((tile, W_A), row),
            pl.BlockSpec(wcat.shape, lambda b, t: (0, 0, 0)),
            pl.BlockSpec((CONV_WIDTH, W_A), const2),
            vec, vec, vec, vec,
        ],
        out_specs=[pl.BlockSpec((tile, W_A), row), tail_spec, tail_spec],
        out_shape=[
            jax.ShapeDtypeStruct((BATCH * SEQ, W_A), BF16),
            jax.ShapeDtypeStruct((BATCH, SUBLANES, W_A), F32),
            jax.ShapeDtypeStruct((BATCH, SUBLANES, W_A), F32),
        ],
        scratch_shapes=[
            pltpu.VMEM((SUBLANES, W_A), F32),
            pltpu.VMEM((SUBLANES, W_A), F32),
            pltpu.VMEM((tile, W_A), F32),
            pltpu.VMEM((tile, W_A), F32),
            pltpu.VMEM((tile, W_A), F32),
        ],
        compiler_params=_cparams("parallel", "arbitrary"),
        name="lru_prompt",
    )(u, gate, wcat, cw, cb, gab, gxb, lam)


def _lru_sample_kernel(u_ref, gate_ref, prev_ref, h0_ref, wcat_ref, cw_ref, cb_ref, gab_ref, gxb_ref, lam_ref,
                       y_ref, h_ref, a_scr, b_scr):
    xc = _conv_rows(u_ref[...], prev_ref[...], cw_ref, cb_ref)
    _lru_coeffs(xc, wcat_ref, gab_ref, gxb_ref, lam_ref, None, a_scr, b_scr)
    a = a_scr[...]
    _, h = _scan_within_tiles(a, b_scr[...] + a * h0_ref[...])
    h_ref[...] = h
    y_ref[...] = (_gelu_tanh(gate_ref[...]) * h).astype(y_ref.dtype)


def _lru_sample(u, gate, prev, h0pad, wcat, cw, cb, gab, gxb, lam):
    m = u.shape[0]
    tile = 256
    row = lambda i: (i, 0)
    const2 = lambda i: (0, 0)
    vec = pl.BlockSpec((1, W_A), const2)
    blk = pl.BlockSpec((tile, W_A), row)
    return pl.pallas_call(
        _lru_sample_kernel,
        grid=(m // tile,),
        in_specs=[blk, blk, blk, blk,
                  pl.BlockSpec(wcat.shape, lambda i: (0, 0, 0)),
                  pl.BlockSpec((CONV_WIDTH, W_A), const2),
                  vec, vec, vec, vec],
        out_specs=[blk, blk],
        out_shape=[jax.ShapeDtypeStruct((m, W_A), BF16), jax.ShapeDtypeStruct((m, W_A), F32)],
        scratch_shapes=[pltpu.VMEM((tile, W_A), F32), pltpu.VMEM((tile, W_A), F32)],
        compiler_params=_cparams("parallel"),
        name="lru_sample",
    )(u, gate, prev, h0pad, wcat, cw, cb, gab, gxb, lam)


def _bias_tile_kernel(rel_ref, o_ref):
    h = pl.program_id(0)
    w = pl.program_id(1)
    shape = (MOBA_BLOCK, MOBA_BLOCK)
    r = lax.broadcasted_iota(jnp.int32, shape, 0)
    c = lax.broadcasted_iota(jnp.int32, shape, 1)
    dist = jnp.maximum(w * MOBA_BLOCK + c - r, 0)
    large = jnp.full(shape, REL_BUCKETS // 2, jnp.int32)
    for thr in T5_THRESHOLDS:
        large = large + (dist >= thr).astype(jnp.int32)
    bucket = jnp.where(dist < REL_BUCKETS // 2, dist, large)
    out = jnp.zeros(shape, F32)
    for b in range(REL_BUCKETS):
        out = jnp.where(bucket == b, rel_ref[b, h], out)
    o_ref[...] = out


def _bias_tiles(rel_bias):
    return pl.pallas_call(
        _bias_tile_kernel,
        grid=(B_HEADS, 2),
        in_specs=[pl.BlockSpec(memory_space=pltpu.SMEM)],
        out_specs=pl.BlockSpec((None, None, MOBA_BLOCK, MOBA_BLOCK), lambda h, w: (h, w, 0, 0)),
        out_shape=jax.ShapeDtypeStruct((B_HEADS, 2, MOBA_BLOCK, MOBA_BLOCK), F32),
        compiler_params=_cparams("parallel", "parallel"),
        name="bias_tiles",
    )(rel_bias)


def _select_topk(gate, n_past):
    blk = lax.broadcasted_iota(jnp.int32, gate.shape, 1)
    past = blk < n_past
    g = jnp.where(past, gate, NEG_INF)
    rank = jnp.zeros(gate.shape, jnp.int32)
    for m in range(gate.shape[1]):
        gm = g[:, m:m + 1]
        beats = jnp.logical_or(gm > g, jnp.logical_and(gm == g, blk > m))
        rank = rank + beats.astype(jnp.int32)
    return jnp.where(jnp.logical_and(past, rank < MOBA_TOPK), 1.0, 0.0)


def _select_topk_t(gate_t, n_past):
    blk = lax.broadcasted_iota(jnp.int32, gate_t.shape, 0)
    past = blk < n_past
    g = jnp.where(past, gate_t, NEG_INF)
    rank = jnp.zeros(gate_t.shape, jnp.int32)
    for m in range(gate_t.shape[0]):
        gm = g[m:m + 1, :]
        beats = jnp.logical_or(gm > g, jnp.logical_and(gm == g, blk > m))
        rank = rank + beats.astype(jnp.int32)
    return jnp.where(jnp.logical_and(past, rank < MOBA_TOPK), 1.0, 0.0)


def _moba_prompt_kernel(rel_ref, q_ref, k_ref, v_ref, kmean_ref, bias_ref, o_ref, vt_scr, s_scr):
    h = pl.program_id(1)
    scale = B_HEAD_DIM ** -0.5

    for n in range(N_BLOCKS):
        cols = slice(n * MOBA_BLOCK, (n + 1) * MOBA_BLOCK)
        vt_scr[:, cols] = v_ref[cols, :].astype(F32).T.astype(BF16)

    gate_all = _dot_nt(kmean_ref[...], q_ref[...], precision=lax.Precision.HIGHEST)
    query_block = lax.broadcasted_iota(jnp.int32, gate_all.shape, 1) // MOBA_BLOCK
    penalty_all = jnp.where(_select_topk_t(gate_all, query_block) > 0.0, 0.0, NEG_INF)
    far_bias = rel_ref[REL_BUCKETS - 1, h]
    key = lax.broadcasted_iota(jnp.int32, (MOBA_BLOCK, MOBA_BLOCK), 0)
    query = lax.broadcasted_iota(jnp.int32, (MOBA_BLOCK, MOBA_BLOCK), 1)

    def tile_rows(x, op):
        return op(x.reshape(MOBA_BLOCK // SUBLANES, SUBLANES, MOBA_BLOCK), axis=0)

    def attend(own):
        q_rows = slice(own * MOBA_BLOCK, (own + 1) * MOBA_BLOCK)
        qb = q_ref[q_rows, :].astype(BF16)
        penalty_t = penalty_all[:, q_rows]
        first_tile = own * (own + 1) // 2
        m_part = None
        for n in range(own, -1, -1):
            rows = slice(n * MOBA_BLOCK, (n + 1) * MOBA_BLOCK)
            s = _dot_nt(k_ref[rows, :], qb) * scale
            if n == own:
                s = jnp.where(key <= query, s + bias_ref[0], NEG_INF)
            elif n == own - 1:
                s = s + bias_ref[1] + penalty_t[n:n + 1, :]
            else:
                s = s + (far_bias + penalty_t[n:n + 1, :])
            s_scr[first_tile + n] = s
            part = tile_rows(s, jnp.max)
            m_part = part if m_part is None else jnp.maximum(m_part, part)
        m = jnp.max(m_part, axis=0, keepdims=True)
        l_part = jnp.zeros((SUBLANES, MOBA_BLOCK), F32)
        acc = jnp.zeros((B_HEAD_DIM, MOBA_BLOCK), F32)
        for n in range(own + 1):
            rows = slice(n * MOBA_BLOCK, (n + 1) * MOBA_BLOCK)
            p = jnp.exp(s_scr[first_tile + n] - m)
            l_part = l_part + tile_rows(p, jnp.sum)
            acc = acc + _dot(vt_scr[:, rows], p.astype(BF16))
        l = jnp.sum(l_part, axis=0, keepdims=True)
        o_ref[q_rows, :] = (acc / l).T.astype(o_ref.dtype)

    for own in range(N_BLOCKS):
        attend(own)


def _moba_prompt(q, k_bf, v_bf, kmean, bias_tiles, rel_bias):
    batch = q.shape[0] // SEQ
    seq_spec = pl.BlockSpec((SEQ, B_HEAD_DIM), lambda b, h: (b, h))
    n_tiles = N_BLOCKS * (N_BLOCKS + 1) // 2
    return pl.pallas_call(
        _moba_prompt_kernel,
        grid=(batch, B_HEADS),
        in_specs=[
            pl.BlockSpec(memory_space=pltpu.SMEM),
            seq_spec, seq_spec, seq_spec,
            pl.BlockSpec((None, N_BLOCKS, B_HEAD_DIM), lambda b, h: (b, 0, h)),
            pl.BlockSpec((None, 2, MOBA_BLOCK, MOBA_BLOCK), lambda b, h: (h, 0, 0, 0)),
        ],
        out_specs=seq_spec,
        out_shape=jax.ShapeDtypeStruct((batch * SEQ, W_B), BF16),
        scratch_shapes=[pltpu.VMEM((B_HEAD_DIM, SEQ), BF16),
                        pltpu.VMEM((n_tiles, MOBA_BLOCK, MOBA_BLOCK), F32)],
        compiler_params=_cparams("parallel", "parallel"),
        name="moba_prompt",
    )(rel_bias, q, k_bf, v_bf, kmean, bias_tiles)


SAMPLE_ROWS = DEC_SEQ * B_HEADS
PAGE_ROWS = PAGE_SIZE * B_HEADS


def _moba_sample_kernel(pt_ref, q_ref, kn_ref, vn_ref, bprev_ref, bnew_ref, far_ref, *refs):
    del pt_ref
    k_pages = refs[:N_PAGES]
    v_pages = refs[N_PAGES:2 * N_PAGES]
    o_ref = refs[2 * N_PAGES]
    s_scr = refs[2 * N_PAGES + 1]
    scale = B_HEAD_DIM ** -0.5
    pages_per_block = MOBA_BLOCK // PAGE_SIZE
    n_past = PAST_LEN // MOBA_BLOCK

    q = q_ref[...]
    qb = q.astype(BF16)
    q3 = q.reshape(DEC_SEQ, B_HEADS, B_HEAD_DIM)
    gate_cols = []
    for n in range(n_past):
        ksum = sum(k_pages[n * pages_per_block + j][...].reshape(PAGE_SIZE, B_HEADS, B_HEAD_DIM).sum(axis=0)
                   for j in range(pages_per_block))
        kmean = ksum * (1.0 / MOBA_BLOCK)
        gate_cols.append(jnp.sum(q3 * kmean[None], axis=-1, keepdims=True).reshape(SAMPLE_ROWS, 1))
    gate = jnp.concatenate(gate_cols, axis=1)
    penalty = jnp.where(_select_topk(gate, n_past) > 0.0, 0.0, NEG_INF)

    def same_head_penalty(cols):
        r = lax.broadcasted_iota(jnp.int32, (SAMPLE_ROWS, cols), 0) % B_HEADS
        c = lax.broadcasted_iota(jnp.int32, (SAMPLE_ROWS, cols), 1) % B_HEADS
        return jnp.where(r == c, 0.0, NEG_INF)

    head_pen = same_head_penalty(PAGE_ROWS)
    far_bias = far_ref[...]
    m_el = jnp.full((SAMPLE_ROWS, PAGE_ROWS), NEG_INF, F32)
    for p in range(N_PAGES):
        n = p // pages_per_block
        s = _dot_nt(qb, k_pages[p][...].astype(BF16)) * scale
        if n == n_past - 1:
            j = p - n * pages_per_block
            s = s + bprev_ref[:, j * PAGE_ROWS:(j + 1) * PAGE_ROWS] + penalty[:, n:n + 1]
        else:
            s = s + (far_bias + penalty[:, n:n + 1])
        s = s + head_pen
        s_scr[p] = s
        m_el = jnp.maximum(m_el, s)

    s_new = _dot_nt(q, kn_ref[...]) * scale + bnew_ref[...] + same_head_penalty(SAMPLE_ROWS)
    t_q = lax.broadcasted_iota(jnp.int32, (SAMPLE_ROWS, SAMPLE_ROWS), 0) // B_HEADS
    t_k = lax.broadcasted_iota(jnp.int32, (SAMPLE_ROWS, SAMPLE_ROWS), 1) // B_HEADS
    s_new = jnp.where(t_k <= t_q, s_new, NEG_INF)

    m = jnp.maximum(jnp.max(m_el, axis=-1, keepdims=True), jnp.max(s_new, axis=-1, keepdims=True))
    p_new = jnp.exp(s_new - m)
    acc = _dot(p_new, vn_ref[...])
    l_el = jnp.zeros((SAMPLE_ROWS, PAGE_ROWS), F32)
    for p in range(N_PAGES):
        pr = jnp.exp(s_scr[p] - m)
        l_el = l_el + pr
        acc = acc + _dot(pr.astype(BF16), v_pages[p][...].astype(BF16))
    l = jnp.sum(p_new, axis=-1, keepdims=True) + jnp.sum(l_el, axis=-1, keepdims=True)
    o_ref[...] = acc / l


def _moba_sample(q_rows, k_rows, v_rows, cache_k_rows, cache_v_rows, layer, page_table_flat,
                 bias_prev, bias_new, far_col):
    n_seq = q_rows.shape[0] // SAMPLE_ROWS
    tok_spec = pl.BlockSpec((SAMPLE_ROWS, B_HEAD_DIM), lambda b, pt: (b, 0))

    def page_spec(p):
        return pl.BlockSpec((None, None, PAGE_ROWS, B_HEAD_DIM),
                            lambda b, pt: (layer, pt[b * N_PAGES + p], 0, 0))

    def const_spec(a):
        return pl.BlockSpec(a.shape, lambda b, pt: (0, 0))

    pages = [page_spec(p) for p in range(N_PAGES)]
    grid_spec = pltpu.PrefetchScalarGridSpec(
        num_scalar_prefetch=1,
        grid=(n_seq,),
        in_specs=[tok_spec, tok_spec, tok_spec, const_spec(bias_prev), const_spec(bias_new),
                  const_spec(far_col)] + pages + pages,
        out_specs=tok_spec,
        scratch_shapes=[pltpu.VMEM((N_PAGES, SAMPLE_ROWS, PAGE_ROWS), F32)],
    )
    return pl.pallas_call(
        _moba_sample_kernel,
        grid_spec=grid_spec,
        out_shape=jax.ShapeDtypeStruct((n_seq * SAMPLE_ROWS, B_HEAD_DIM), F32),
        compiler_params=_cparams("parallel"),
        name="moba_sample",
    )(page_table_flat, q_rows, k_rows, v_rows, bias_prev, bias_new, far_col,
      *([cache_k_rows] * N_PAGES), *([cache_v_rows] * N_PAGES))


def kernel(x_prompt, x_sample, cache_k, cache_v, state_conv, state_h, cache_mem_k, cache_mem_v, page_table,
           mem_prompt, norm_mix, norm_mlp, norm_mem, norm_final, w_in_a, w_out_a, conv_w, conv_b, gate_a_w,
           gate_a_b, gate_x_w, gate_x_b, lru_lambda, w_in_b, w_out_b, rel_bias, w_mem_kv, w_up, w_down):
    n_prompt = BATCH * SEQ
    n_sample = DEC_BATCH * DEC_SEQ
    xp = x_prompt.reshape(n_prompt, D_MODEL)
    xs = x_sample.reshape(n_sample, D_MODEL)

    w_in_a_bf, w_in_b_bf = w_in_a.astype(BF16), w_in_b.astype(BF16)
    w_out_a_bf, w_out_b_bf = w_out_a.astype(BF16), w_out_b.astype(BF16)
    w_up_bf, w_down_bf = w_up.astype(BF16), w_down.astype(BF16)
    wcat_bf = jnp.concatenate([gate_a_w, gate_x_w], axis=-1).astype(BF16)
    gf = norm_final.reshape(1, D_MODEL)

    memk, memv = _mem_kv(mem_prompt.reshape(BATCH * MEM_LEN, D_MODEL), norm_mem.reshape(DEPTH, 1, D_MODEL),
                         w_mem_kv.astype(BF16))
    cmk = cache_mem_k.reshape(DEPTH, DEC_BATCH, XMEM_ROWS, X_HEAD_DIM)
    cmv = cache_mem_v.reshape(DEPTH, DEC_BATCH, XMEM_ROWS, X_HEAD_DIM)
    cache_k_rows = cache_k.reshape(cache_k.shape[0], cache_k.shape[1], PAGE_ROWS, B_HEAD_DIM)
    cache_v_rows = cache_v.reshape(cache_v.shape[0], cache_v.shape[1], PAGE_ROWS, B_HEAD_DIM)
    pt_flat = page_table.reshape(DEC_BATCH * N_PAGES)

    tiles = _bias_tiles(rel_bias)
    bias_prev = jnp.transpose(tiles[:, 1, :, :DEC_SEQ], (2, 1, 0)).reshape(DEC_SEQ, 1, MOBA_BLOCK * B_HEADS)
    bias_prev = jnp.broadcast_to(bias_prev, (DEC_SEQ, B_HEADS, MOBA_BLOCK * B_HEADS))
    bias_prev = bias_prev.reshape(SAMPLE_ROWS, MOBA_BLOCK * B_HEADS)
    bias_new = jnp.transpose(tiles[:, 0, :DEC_SEQ, :DEC_SEQ], (2, 1, 0)).reshape(DEC_SEQ, 1, SAMPLE_ROWS)
    bias_new = jnp.broadcast_to(bias_new, (DEC_SEQ, B_HEADS, SAMPLE_ROWS)).reshape(SAMPLE_ROWS, SAMPLE_ROWS)
    far_col = jnp.tile(rel_bias[REL_BUCKETS - 1], DEC_SEQ).reshape(SAMPLE_ROWS, 1)

    kp_l, vp_l, ks_l, vs_l = [], [], [], []
    cp_l, hp_l, cs_l, hs_l = [], [], [], []
    for i in range(DEPTH):
        j = i // 2
        g_mix = norm_mix[i].reshape(1, D_MODEL)
        g_mlp = norm_mlp[i].reshape(1, D_MODEL)
        final = i == DEPTH - 1
        if i % 2 == 0:
            lru_w = (wcat_bf[j], conv_w[j], conv_b[j].reshape(1, W_A), gate_a_b[j].reshape(1, W_A),
                     gate_x_b[j].reshape(1, W_A), lru_lambda[j].reshape(1, W_A))
            w_out = w_out_a_bf

            def outs_a(qx_kind, qx_dtype):
                return ((0, W_A, "tok", F32), (W_A, W_A, "tok", F32), (2 * W_A, W_X, qx_kind, qx_dtype))

            u, gate, qx = _norm_proj(xp, g_mix, w_in_a_bf, j, outs_a("tok", BF16))
            y_p, utail, htail = _lru_prompt(u, gate, *lru_w)
            cp_l.append(utail[:, SUBLANES - (CONV_WIDTH - 1):, :])
            hp_l.append(htail[:, SUBLANES - 1, :])

            u, gate, qx_s = _norm_proj(xs, g_mix, w_in_a_bf, j, outs_a("rows", F32))
            prev = jnp.pad(state_conv[j], ((0, 0), (SUBLANES - (CONV_WIDTH - 1), 0), (0, 0)))
            h0pad = jnp.pad(state_h[j][:, None, :], ((0, 0), (0, DEC_SEQ - 1), (0, 0)))
            y_s, h_s = _lru_sample(u, gate, prev.reshape(n_sample, W_A), h0pad.reshape(n_sample, W_A), *lru_w)
            cs_l.append(u.reshape(DEC_BATCH, DEC_SEQ, W_A)[:, DEC_SEQ - (CONV_WIDTH - 1):, :])
            hs_l.append(h_s.reshape(DEC_BATCH, DEC_SEQ, W_A)[:, DEC_SEQ - 1, :])
        else:
            w_out = w_out_b_bf

            outs_p = ((0, W_B, "tok", F32),
                      (W_B, W_B, "rows", F32), (W_B, W_B, "tok", BF16), (W_B, W_B, "blockmean", F32),
                      (2 * W_B, W_B, "rows", F32), (2 * W_B, W_B, "tok", BF16),
                      (3 * W_B, W_X, "tok", BF16))
            q, k_rows, k_bf, kmean, v_rows, v_bf, qx = _norm_proj(xp, g_mix, w_in_b_bf, j, outs_p)
            y_p = _moba_prompt(q, k_bf, v_bf, kmean.reshape(BATCH, N_BLOCKS, W_B), tiles, rel_bias)
            kp_l.append(k_rows.reshape(BATCH, SEQ, B_HEADS, B_HEAD_DIM))
            vp_l.append(v_rows.reshape(BATCH, SEQ, B_HEADS, B_HEAD_DIM))

            outs_s = ((0, W_B, "rows", F32), (W_B, W_B, "rows", F32), (2 * W_B, W_B, "rows", F32),
                      (3 * W_B, W_X, "rows", F32))
            q_rows, k_rows, v_rows, qx_s = _norm_proj(xs, g_mix, w_in_b_bf, j, outs_s)
            y_s = _moba_sample(q_rows, k_rows, v_rows, cache_k_rows, cache_v_rows, j, pt_flat,
                               bias_prev, bias_new, far_col).reshape(n_sample, W_B)
            ks_l.append(k_rows.reshape(DEC_BATCH, DEC_SEQ, B_HEADS, B_HEAD_DIM))
            vs_l.append(v_rows.reshape(DEC_BATCH, DEC_SEQ, B_HEADS, B_HEAD_DIM))

        ca_p = _xattn_prompt(qx, memk, memv, i)
        ca_s = _xattn_sample(qx_s, cmk, cmv, i).reshape(n_sample, W_X)
        post_w = (w_out, j, g_mlp, w_up_bf, w_down_bf, i, gf, final)
        xp = _post(xp, y_p, ca_p, *post_w)
        xs = _post(xs, y_s, ca_s, *post_w)

    mem_shape = (DEPTH, BATCH, MEM_LEN, X_HEADS, X_HEAD_DIM)
    return (xp.reshape(BATCH, SEQ, D_MODEL), xs.reshape(DEC_BATCH, DEC_SEQ, D_MODEL),
            jnp.stack(kp_l), jnp.stack(vp_l), jnp.stack(ks_l), jnp.stack(vs_l),
            jnp.stack(cp_l), jnp.stack(hp_l), jnp.stack(cs_l), jnp.stack(hs_l),
            memk.reshape(mem_shape), memv.reshape(mem_shape))
```

```python
import functools
import math

import jax
import jax.numpy as jnp
from jax import lax
from jax.experimental import pallas as pl
from jax.experimental.pallas import tpu as pltpu

F32 = jnp.float32
BF16 = jnp.bfloat16

D_MODEL = 1024
BATCH = 8
SEQ = 2048
DEPTH = 4
DEC_BATCH = 128
DEC_SEQ = 8
PAST_LEN = 2048
PAGE_SIZE = 128
N_PAGES = PAST_LEN // PAGE_SIZE
W_A = D_MODEL
LRU_BLOCKS = 8
LRU_BLOCK_DIM = W_A // LRU_BLOCKS
CONV_WIDTH = 4
LRU_C = 8.0
B_HEADS = 8
B_HEAD_DIM = D_MODEL // B_HEADS
W_B = B_HEADS * B_HEAD_DIM
MOBA_BLOCK = 256
MOBA_TOPK = 3
N_BLOCKS = SEQ // MOBA_BLOCK
REL_BUCKETS = 32
REL_MAX_DIST = 128
X_HEADS = 4
X_HEAD_DIM = 128
W_X = X_HEADS * X_HEAD_DIM
MEM_LEN = 256
D_FF = 4 * D_MODEL
EPS = 1e-6

SUBLANES = 8
VMEM_LIMIT = 56 * 1024 * 1024
NEG_INF = float("-inf")


def _cparams(*sem):
    return pltpu.CompilerParams(dimension_semantics=sem, vmem_limit_bytes=VMEM_LIMIT)


def _dot(a, b):
    return jnp.dot(a, b, preferred_element_type=F32)


def _dot_nt(a, b, precision=None):
    return lax.dot_general(a, b, (((1,), (1,)), ((), ())), preferred_element_type=F32, precision=precision)


def _rms(x, g):
    return x * lax.rsqrt(jnp.mean(x * x, axis=-1, keepdims=True) + EPS) * g


def _t5_thresholds():
    max_exact = REL_BUCKETS // 2
    thr = []
    for k in range(1, REL_BUCKETS - max_exact):
        d = max_exact
        while int(math.log(d / max_exact) / math.log(REL_MAX_DIST / max_exact) * (REL_BUCKETS - max_exact)) < k:
            d += 1
        thr.append(d)
    return thr


T5_THRESHOLDS = _t5_thresholds()


def _mem_kv_kernel(x_ref, g_ref, w_ref, k_rows_ref, v_rows_ref, k_bf_ref, v_bf_ref):
    tm = x_ref.shape[0]
    hn = _rms(x_ref[...], g_ref[...]).astype(BF16)
    r = _dot(hn, w_ref[...])
    for rows_ref, bf_ref, off in ((k_rows_ref, k_bf_ref, 0), (v_rows_ref, v_bf_ref, W_X)):
        bf_ref[...] = r[:, off:off + W_X].astype(BF16)
        for h in range(X_HEADS):
            rows_ref[pl.ds(h, tm, stride=X_HEADS), :] = r[:, off + h * X_HEAD_DIM:off + (h + 1) * X_HEAD_DIM]


def _mem_kv(mem2d, norm_mem, w_bf):
    m = mem2d.shape[0]
    tm = 512
    rows_spec = pl.BlockSpec((None, tm * X_HEADS, X_HEAD_DIM), lambda l, i: (l, i, 0))
    tok_spec = pl.BlockSpec((None, tm, W_X), lambda l, i: (l, i, 0))
    rows_shape = jax.ShapeDtypeStruct((DEPTH, m * X_HEADS, X_HEAD_DIM), F32)
    tok_shape = jax.ShapeDtypeStruct((DEPTH, m, W_X), BF16)
    return pl.pallas_call(
        _mem_kv_kernel,
        grid=(DEPTH, m // tm),
        in_specs=[
            pl.BlockSpec((tm, D_MODEL), lambda l, i: (i, 0)),
            pl.BlockSpec((None, 1, D_MODEL), lambda l, i: (l, 0, 0)),
            pl.BlockSpec((None, D_MODEL, 2 * W_X), lambda l, i: (l, 0, 0)),
        ],
        out_specs=[rows_spec, rows_spec, tok_spec, tok_spec],
        out_shape=[rows_shape, rows_shape, tok_shape, tok_shape],
        compiler_params=_cparams("parallel", "parallel"),
        name="mem_kv",
    )(mem2d, norm_mem, w_bf)


LANES = 128


def _norm_proj_kernel(x_ref, g_ref, w_ref, *refs, outs):
    n_earlier = sum(kind == "rows2" for _, _, kind, _ in outs)
    earlier_refs = list(refs[:n_earlier])
    out_refs = refs[n_earlier:]
    tm = x_ref.shape[0]
    for c in range(tm // MOBA_BLOCK):
        rows = slice(c * MOBA_BLOCK, (c + 1) * MOBA_BLOCK)
        hn = _rms(x_ref[rows, :], g_ref[...]).astype(BF16)
        results = {}
        earlier = iter(earlier_refs)
        for o_ref, (off, n, kind, _) in zip(out_refs, outs):
            if off not in results:
                results[off] = _dot(hn, w_ref[:, off:off + n])
            r = results[off]
            heads = n // LANES
            head_rows = slice(c * MOBA_BLOCK * heads, (c + 1) * MOBA_BLOCK * heads)
            if kind == "tok":
                o_ref[rows, :] = r.astype(o_ref.dtype)
            elif kind == "rows":
                for h in range(heads):
                    o_ref[pl.ds(head_rows.start + h, MOBA_BLOCK, stride=heads), :] = r[:, h * LANES:(h + 1) * LANES]
            elif kind == "rows2":
                o_ref[0, head_rows, :] = next(earlier)[head_rows, :]
                for h in range(heads):
                    o_ref[1, pl.ds(head_rows.start + h, MOBA_BLOCK, stride=heads), :] = (
                        r[:, h * LANES:(h + 1) * LANES])
            else:
                o_ref[c] = jnp.mean(r, axis=0, keepdims=True)


def _norm_proj(x, g, w_bf, mixer_layer, outs, earlier=()):
    m = x.shape[0]
    tm = 512
    n_all = w_bf.shape[2]
    specs, shapes, earlier_specs = [], [], []
    for _, n, kind, dt in outs:
        heads = n // LANES
        if kind == "tok":
            specs.append(pl.BlockSpec((tm, n), lambda i: (i, 0)))
            shapes.append(jax.ShapeDtypeStruct((m, n), dt))
        elif kind == "rows":
            specs.append(pl.BlockSpec((tm * heads, LANES), lambda i: (i, 0)))
            shapes.append(jax.ShapeDtypeStruct((m * heads, LANES), dt))
        elif kind == "rows2":
            earlier_specs.append(pl.BlockSpec((tm * heads, LANES), lambda i: (i, 0)))
            specs.append(pl.BlockSpec((2, tm * heads, LANES), lambda i: (0, i, 0)))
            shapes.append(jax.ShapeDtypeStruct((2, m * heads, LANES), dt))
        else:
            specs.append(pl.BlockSpec((tm // MOBA_BLOCK, 1, n), lambda i: (i, 0, 0)))
            shapes.append(jax.ShapeDtypeStruct((m // MOBA_BLOCK, 1, n), dt))
    return pl.pallas_call(
        functools.partial(_norm_proj_kernel, outs=outs),
        grid=(m // tm,),
        in_specs=[
            pl.BlockSpec((tm, D_MODEL), lambda i: (i, 0)),
            pl.BlockSpec((1, D_MODEL), lambda i: (0, 0)),
            pl.BlockSpec((None, D_MODEL, n_all), lambda i: (mixer_layer, 0, 0), pipeline_mode=pl.Buffered(1)),
        ] + earlier_specs,
        out_specs=specs,
        out_shape=shapes,
        compiler_params=_cparams("parallel"),
        name="norm_proj",
    )(x, g, w_bf, *earlier)


def _softmax_pv(s, v_bf):
    m = jnp.max(s, axis=-1, keepdims=True)
    p = jnp.exp(s - m)
    l = jnp.sum(p, axis=-1, keepdims=True)
    return _dot(p.astype(BF16), v_bf) / l


def _xattn_prompt_kernel(q_ref, k_ref, v_ref, o_ref):
    scale = X_HEAD_DIM ** -0.5
    for h in range(X_HEADS):
        sl = slice(h * X_HEAD_DIM, (h + 1) * X_HEAD_DIM)
        s = _dot_nt(q_ref[:, sl], k_ref[:, sl]) * scale
        o_ref[:, sl] = _softmax_pv(s, v_ref[:, sl]).astype(o_ref.dtype)


def _xattn_prompt(qx, memk, memv, layer):
    tq = 512
    nt = SEQ // tq
    kv_spec = pl.BlockSpec((None, MEM_LEN, W_X), lambda b, t: (layer, b, 0))
    return pl.pallas_call(
        _xattn_prompt_kernel,
        grid=(BATCH, nt),
        in_specs=[pl.BlockSpec((tq, W_X), lambda b, t: (b * nt + t, 0)), kv_spec, kv_spec],
        out_specs=pl.BlockSpec((tq, W_X), lambda b, t: (b * nt + t, 0)),
        out_shape=jax.ShapeDtypeStruct((BATCH * SEQ, W_X), BF16),
        compiler_params=_cparams("parallel", "parallel"),
        name="xattn_prompt",
    )(qx, memk, memv)


XQ_ROWS = DEC_SEQ * X_HEADS
XMEM_ROWS = MEM_LEN * X_HEADS


def _xattn_sample_kernel(q_ref, k_ref, v_ref, o_ref, *, group):
    scale = X_HEAD_DIM ** -0.5
    r = lax.broadcasted_iota(jnp.int32, (XQ_ROWS, XMEM_ROWS), 0) % X_HEADS
    c = lax.broadcasted_iota(jnp.int32, (XQ_ROWS, XMEM_ROWS), 1) % X_HEADS
    head_pen = jnp.where(r == c, 0.0, NEG_INF)
    for g in range(group):
        rows = slice(g * XQ_ROWS, (g + 1) * XQ_ROWS)
        s = _dot_nt(q_ref[rows, :].astype(BF16), k_ref[g].astype(BF16)) * scale + head_pen
        o_ref[rows, :] = _softmax_pv(s, v_ref[g].astype(BF16))


def _xattn_sample(q_rows, cmk_rows, cmv_rows, layer):
    group = 8
    n_seq = q_rows.shape[0] // XQ_ROWS
    kv_spec = pl.BlockSpec((None, group, XMEM_ROWS, X_HEAD_DIM), lambda i: (layer, i, 0, 0))
    q_spec = pl.BlockSpec((group * XQ_ROWS, X_HEAD_DIM), lambda i: (i, 0))
    return pl.pallas_call(
        functools.partial(_xattn_sample_kernel, group=group),
        grid=(n_seq // group,),
        in_specs=[q_spec, kv_spec, kv_spec],
        out_specs=q_spec,
        out_shape=jax.ShapeDtypeStruct((n_seq * XQ_ROWS, X_HEAD_DIM), F32),
        compiler_params=_cparams("parallel"),
        name="xattn_sample",
    )(q_rows, cmk_rows, cmv_rows)


POST_ROW_CHUNK = 256


def _post_kernel(x_ref, y_ref, c_ref, wy_ref, wc_ref, g_ref, wu_ref, wd_ref, gf_ref, o_ref, hn_scr, *, final_norm):
    f = pl.program_id(1)
    tm = o_ref.shape[0]

    @pl.when(f == 0)
    def _():
        for c in range(tm // POST_ROW_CHUNK):
            rows = slice(c * POST_ROW_CHUNK, (c + 1) * POST_ROW_CHUNK)
            xn = (x_ref[rows, :] + _dot(y_ref[rows, :].astype(BF16), wy_ref[...])
                  + _dot(c_ref[rows, :].astype(BF16), wc_ref[...]))
            o_ref[rows, :] = xn
            hn_scr[rows, :] = _rms(xn, g_ref[...]).astype(BF16)

    h = _dot(hn_scr[...], wu_ref[...])
    h = jnp.square(jnp.maximum(h, 0.0)).astype(BF16)
    o_ref[...] += _dot(h, wd_ref[...])

    if final_norm:
        @pl.when(f == pl.num_programs(1) - 1)
        def _():
            o_ref[...] = _rms(o_ref[...], gf_ref[...])


def _post(x, y, ca, w_out, mixer_layer, g, w_up, w_down, layer, gf, final_norm):
    m = x.shape[0]
    tm, tf = 1024, 1024
    mix_w = y.shape[1]
    row = lambda i, f: (i, 0)
    const = lambda i, f: (0, 0)
    return pl.pallas_call(
        functools.partial(_post_kernel, final_norm=final_norm),
        grid=(m // tm, D_FF // tf),
        in_specs=[
            pl.BlockSpec((tm, D_MODEL), row),
            pl.BlockSpec((tm, mix_w), row),
            pl.BlockSpec((tm, W_X), row),
            pl.BlockSpec((None, mix_w, D_MODEL), lambda i, f: (mixer_layer, 0, 0)),
            pl.BlockSpec((None, W_X, D_MODEL), lambda i, f: (mixer_layer, mix_w // W_X, 0)),
            pl.BlockSpec((1, D_MODEL), const),
            pl.BlockSpec((None, D_MODEL, tf), lambda i, f: (layer, 0, f)),
            pl.BlockSpec((None, tf, D_MODEL), lambda i, f: (layer, f, 0)),
            pl.BlockSpec((1, D_MODEL), const),
        ],
        out_specs=pl.BlockSpec((tm, D_MODEL), row),
        out_shape=jax.ShapeDtypeStruct((m, D_MODEL), F32),
        scratch_shapes=[pltpu.VMEM((tm, D_MODEL), BF16)],
        compiler_params=_cparams("parallel", "arbitrary"),
        name="post",
    )(x, y, ca, w_out, w_out, g, w_up, w_down, gf)


def _gelu_tanh(x):
    return x * (0.5 * (1.0 + jnp.tanh(math.sqrt(2.0 / math.pi) * (x + 0.044715 * (x * x * x)))))


def _conv_rows(u, u_prev, cw_ref, cb_ref):
    rows = u.shape[0]
    g = rows // SUBLANES
    u3 = u.reshape(g, SUBLANES, W_A)
    p3 = u_prev.reshape(g, SUBLANES, W_A)
    sub = lax.broadcasted_iota(jnp.int32, (g, SUBLANES, W_A), 1)
    xc = cb_ref[...] + cw_ref[CONV_WIDTH - 1:CONV_WIDTH, :] * u3
    for j in range(1, CONV_WIDTH):
        sh = jnp.where(sub >= j, pltpu.roll(u3, j, axis=1), pltpu.roll(p3, j, axis=1))
        xc = xc + cw_ref[CONV_WIDTH - 1 - j:CONV_WIDTH - j, :] * sh
    return xc.reshape(rows, W_A)


def _lru_coeffs(xc, wcat_ref, gab_ref, gxb_ref, lam_ref, first_row, a_scr, b_scr):
    z = -lam_ref[...]
    softplus = jnp.maximum(z, 0.0) + jnp.log1p(jnp.exp(-jnp.abs(z)))
    for n in range(LRU_BLOCKS):
        sl = slice(n * LRU_BLOCK_DIM, (n + 1) * LRU_BLOCK_DIM)
        xn = xc[:, sl]
        ra = _dot(xn.astype(BF16), wcat_ref[n])
        r = jax.nn.sigmoid(ra[:, :LRU_BLOCK_DIM] + gab_ref[:, sl])
        ig = jax.nn.sigmoid(ra[:, LRU_BLOCK_DIM:] + gxb_ref[:, sl])
        log_a = -LRU_C * r * softplus[:, sl]
        th = jnp.tanh(log_a)
        mult = jnp.sqrt(-2.0 * th / (1.0 - th))
        if first_row is not None:
            mult = jnp.where(first_row, 1.0, mult)
        a_scr[:, sl] = jnp.exp(log_a)
        b_scr[:, sl] = mult * ig * xn


def _scan_within_tiles(a, b):
    rows = a.shape[0]
    g = rows // SUBLANES
    a3 = a.reshape(g, SUBLANES, W_A)
    b3 = b.reshape(g, SUBLANES, W_A)
    sub = lax.broadcasted_iota(jnp.int32, (g, SUBLANES, W_A), 1)
    for s in (1, 2, 4):
        keep = sub >= s
        b3 = jnp.where(keep, a3 * pltpu.roll(b3, s, axis=1) + b3, b3)
        a3 = jnp.where(keep, a3 * pltpu.roll(a3, s, axis=1), a3)
    return a3.reshape(rows, W_A), b3.reshape(rows, W_A)


def _lru_prompt_kernel(u_ref, gate_ref, wcat_ref, cw_ref, cb_ref, gab_ref, gxb_ref, lam_ref,
                       y_ref, utail_ref, htail_ref, tail_scr, carry_scr, a_scr, b_scr, h_scr, *, tile):
    t = pl.program_id(1)

    @pl.when(t == 0)
    def _():
        tail_scr[...] = jnp.zeros_like(tail_scr)
        carry_scr[...] = jnp.zeros_like(carry_scr)

    u = u_ref[...]
    u_prev = jnp.concatenate([tail_scr[...], u[:tile - SUBLANES]], axis=0)
    tail_scr[...] = u[tile - SUBLANES:]
    xc = _conv_rows(u, u_prev, cw_ref, cb_ref)
    row = lax.broadcasted_iota(jnp.int32, (tile, LRU_BLOCK_DIM), 0)
    first_row = jnp.logical_and(row == 0, t == 0)
    _lru_coeffs(xc, wcat_ref, gab_ref, gxb_ref, lam_ref, first_row, a_scr, b_scr)
    a, b = _scan_within_tiles(a_scr[...], b_scr[...])
    a_scr[...] = a
    b_scr[...] = b

    def step(g, carry):
        rows = pl.ds(pl.multiple_of(g * SUBLANES, SUBLANES), SUBLANES)
        h = b_scr[rows, :] + a_scr[rows, :] * carry
        h_scr[rows, :] = h
        return jnp.broadcast_to(h[SUBLANES - 1:SUBLANES, :], (SUBLANES, W_A))

    carry_scr[...] = lax.fori_loop(0, tile // SUBLANES, step, carry_scr[...])
    y_ref[...] = (_gelu_tanh(gate_ref[...]) * h_scr[...]).astype(y_ref.dtype)
    utail_ref[...] = u[tile - SUBLANES:]
    htail_ref[...] = h_scr[tile - SUBLANES:, :]


def _lru_prompt(u, gate, wcat, cw, cb, gab, gxb, lam):
    tile = 256
    nt = SEQ // tile
    row = lambda b, t: (b * nt + t, 0)
    const2 = lambda b, t: (0, 0)
    vec = pl.BlockSpec((1, W_A), const2)
    tail_spec = pl.BlockSpec((None, SUBLANES, W_A), lambda b, t: (b, 0, 0))
    return pl.pallas_call(
        functools.partial(_lru_prompt_kernel, tile=tile),
        grid=(BATCH, nt),
        in_specs=[
            pl.BlockSpec((tile, W_A), row),
            pl.BlockSpec((tile, W_A), row),
            pl.BlockSpec(wcat.shape, lambda b, t: (0, 0, 0)),
            pl.BlockSpec((CONV_WIDTH, W_A), const2),
            vec, vec, vec, vec,
        ],
        out_specs=[pl.BlockSpec((tile, W_A), row), tail_spec, tail_spec],
        out_shape=[
            jax.ShapeDtypeStruct((BATCH * SEQ, W_A), BF16),
            jax.ShapeDtypeStruct((BATCH, SUBLANES, W_A), F32),
            jax.ShapeDtypeStruct((BATCH, SUBLANES, W_A), F32),
        ],
        scratch_shapes=[
            pltpu.VMEM((SUBLANES, W_A), F32),
            pltpu.VMEM((SUBLANES, W_A), F32),
            pltpu.VMEM((tile, W_A), F32),
            pltpu.VMEM((tile, W_A), F32),
            pltpu.VMEM((tile, W_A), F32),
        ],
        compiler_params=_cparams("parallel", "arbitrary"),
        name="lru_prompt",
    )(u, gate, wcat, cw, cb, gab, gxb, lam)


def _lru_sample_kernel(u_ref, gate_ref, prev_ref, h0_ref, wcat_ref, cw_ref, cb_ref, gab_ref, gxb_ref, lam_ref,
                       y_ref, h_ref, a_scr, b_scr):
    xc = _conv_rows(u_ref[...], prev_ref[...], cw_ref, cb_ref)
    _lru_coeffs(xc, wcat_ref, gab_ref, gxb_ref, lam_ref, None, a_scr, b_scr)
    a = a_scr[...]
    _, h = _scan_within_tiles(a, b_scr[...] + a * h0_ref[...])
    h_ref[...] = h
    y_ref[...] = (_gelu_tanh(gate_ref[...]) * h).astype(y_ref.dtype)


def _lru_sample(u, gate, prev, h0pad, wcat, cw, cb, gab, gxb, lam):
    m = u.shape[0]
    tile = 256
    row = lambda i: (i, 0)
    const2 = lambda i: (0, 0)
    vec = pl.BlockSpec((1, W_A), const2)
    blk = pl.BlockSpec((tile, W_A), row)
    return pl.pallas_call(
        _lru_sample_kernel,
        grid=(m // tile,),
        in_specs=[blk, blk, blk, blk,
                  pl.BlockSpec(wcat.shape, lambda i: (0, 0, 0)),
                  pl.BlockSpec((CONV_WIDTH, W_A), const2),
                  vec, vec, vec, vec],
        out_specs=[blk, blk],
        out_shape=[jax.ShapeDtypeStruct((m, W_A), BF16), jax.ShapeDtypeStruct((m, W_A), F32)],
        scratch_shapes=[pltpu.VMEM((tile, W_A), F32), pltpu.VMEM((tile, W_A), F32)],
        compiler_params=_cparams("parallel"),
        name="lru_sample",
    )(u, gate, prev, h0pad, wcat, cw, cb, gab, gxb, lam)


def _bias_tile_kernel(rel_ref, o_ref):
    h = pl.program_id(0)
    w = pl.program_id(1)
    shape = (MOBA_BLOCK, MOBA_BLOCK)
    r = lax.broadcasted_iota(jnp.int32, shape, 0)
    c = lax.broadcasted_iota(jnp.int32, shape, 1)
    dist = jnp.maximum(w * MOBA_BLOCK + c - r, 0)
    large = jnp.full(shape, REL_BUCKETS // 2, jnp.int32)
    for thr in T5_THRESHOLDS:
        large = large + (dist >= thr).astype(jnp.int32)
    bucket = jnp.where(dist < REL_BUCKETS // 2, dist, large)
    out = jnp.zeros(shape, F32)
    for b in range(REL_BUCKETS):
        out = jnp.where(bucket == b, rel_ref[b, h], out)
    o_ref[...] = out


def _bias_tiles(rel_bias):
    return pl.pallas_call(
        _bias_tile_kernel,
        grid=(B_HEADS, 2),
        in_specs=[pl.BlockSpec(memory_space=pltpu.SMEM)],
        out_specs=pl.BlockSpec((None, None, MOBA_BLOCK, MOBA_BLOCK), lambda h, w: (h, w, 0, 0)),
        out_shape=jax.ShapeDtypeStruct((B_HEADS, 2, MOBA_BLOCK, MOBA_BLOCK), F32),
        compiler_params=_cparams("parallel", "parallel"),
        name="bias_tiles",
    )(rel_bias)


def _select_topk(gate, n_past):
    blk = lax.broadcasted_iota(jnp.int32, gate.shape, 1)
    past = blk < n_past
    g = jnp.where(past, gate, NEG_INF)
    rank = jnp.zeros(gate.shape, jnp.int32)
    for m in range(gate.shape[1]):
        gm = g[:, m:m + 1]
        beats = jnp.logical_or(gm > g, jnp.logical_and(gm == g, blk > m))
        rank = rank + beats.astype(jnp.int32)
    return jnp.where(jnp.logical_and(past, rank < MOBA_TOPK), 1.0, 0.0)


def _select_topk_t(gate_t, n_past):
    blk = lax.broadcasted_iota(jnp.int32, gate_t.shape, 0)
    past = blk < n_past
    g = jnp.where(past, gate_t, NEG_INF)
    rank = jnp.zeros(gate_t.shape, jnp.int32)
    for m in range(gate_t.shape[0]):
        gm = g[m:m + 1, :]
        beats = jnp.logical_or(gm > g, jnp.logical_and(gm == g, blk > m))
        rank = rank + beats.astype(jnp.int32)
    return jnp.where(jnp.logical_and(past, rank < MOBA_TOPK), 1.0, 0.0)


LOG2E = math.log2(math.e)


def _moba_prompt_kernel(rel_ref, q_ref, k_ref, v_ref, kmean_ref, bias_ref, o_ref, vt_scr, s_scr, bias_scr):
    h = pl.program_id(1)
    scale = B_HEAD_DIM ** -0.5

    for n in range(N_BLOCKS):
        cols = slice(n * MOBA_BLOCK, (n + 1) * MOBA_BLOCK)
        vt_scr[:, cols] = v_ref[cols, :].astype(F32).T.astype(BF16)

    gate_all = _dot_nt(kmean_ref[...], q_ref[...], precision=lax.Precision.HIGHEST)
    query_block = lax.broadcasted_iota(jnp.int32, gate_all.shape, 1) // MOBA_BLOCK
    penalty_all = jnp.where(_select_topk_t(gate_all, query_block) > 0.0, 0.0, NEG_INF)
    far_bias = rel_ref[REL_BUCKETS - 1, h] * LOG2E
    key = lax.broadcasted_iota(jnp.int32, (MOBA_BLOCK, MOBA_BLOCK), 0)
    query = lax.broadcasted_iota(jnp.int32, (MOBA_BLOCK, MOBA_BLOCK), 1)
    bias_scr[0] = jnp.where(key <= query, bias_ref[0] * LOG2E, NEG_INF)
    bias_scr[1] = bias_ref[1] * LOG2E

    def tile_rows(x, op):
        return op(x.reshape(MOBA_BLOCK // SUBLANES, SUBLANES, MOBA_BLOCK), axis=0)

    def attend(own):
        q_rows = slice(own * MOBA_BLOCK, (own + 1) * MOBA_BLOCK)
        qb = q_ref[q_rows, :].astype(BF16)
        penalty_t = penalty_all[:, q_rows]
        first_tile = own * (own + 1) // 2
        m_part = None
        for n in range(own, -1, -1):
            rows = slice(n * MOBA_BLOCK, (n + 1) * MOBA_BLOCK)
            s = _dot_nt(k_ref[rows, :], qb) * (scale * LOG2E)
            if n == own:
                s = s + bias_scr[0]
            elif n == own - 1:
                s = s + bias_scr[1] + penalty_t[n:n + 1, :]
            else:
                s = s + (far_bias + penalty_t[n:n + 1, :])
            s_scr[first_tile + n] = s
            part = tile_rows(s, jnp.max)
            m_part = part if m_part is None else jnp.maximum(m_part, part)
        m = jnp.max(m_part, axis=0, keepdims=True)
        l_part = jnp.zeros((SUBLANES, MOBA_BLOCK), F32)
        acc = jnp.zeros((B_HEAD_DIM, MOBA_BLOCK), F32)
        for n in range(own + 1):
            rows = slice(n * MOBA_BLOCK, (n + 1) * MOBA_BLOCK)
            p = jnp.exp2(s_scr[first_tile + n] - m)
            l_part = l_part + tile_rows(p, jnp.sum)
            acc = acc + _dot(vt_scr[:, rows], p.astype(BF16))
        l = jnp.sum(l_part, axis=0, keepdims=True)
        o_ref[q_rows, :] = (acc / l).T.astype(o_ref.dtype)

    for own in range(N_BLOCKS):
        attend(own)


def _moba_prompt(q, k_bf, v_bf, kmean, bias_tiles, rel_bias):
    batch = q.shape[0] // SEQ
    seq_spec = pl.BlockSpec((SEQ, B_HEAD_DIM), lambda b, h: (b, h))
    n_tiles = N_BLOCKS * (N_BLOCKS + 1) // 2
    return pl.pallas_call(
        _moba_prompt_kernel,
        grid=(batch, B_HEADS),
        in_specs=[
            pl.BlockSpec(memory_space=pltpu.SMEM),
            seq_spec, seq_spec, seq_spec,
            pl.BlockSpec((None, N_BLOCKS, B_HEAD_DIM), lambda b, h: (b, 0, h)),
            pl.BlockSpec((None, 2, MOBA_BLOCK, MOBA_BLOCK), lambda b, h: (h, 0, 0, 0)),
        ],
        out_specs=seq_spec,
        out_shape=jax.ShapeDtypeStruct((batch * SEQ, W_B), BF16),
        scratch_shapes=[pltpu.VMEM((B_HEAD_DIM, SEQ), BF16),
                        pltpu.VMEM((n_tiles, MOBA_BLOCK, MOBA_BLOCK), F32),
                        pltpu.VMEM((2, MOBA_BLOCK, MOBA_BLOCK), F32)],
        compiler_params=_cparams("parallel", "parallel"),
        name="moba_prompt",
    )(rel_bias, q, k_bf, v_bf, kmean, bias_tiles)


SAMPLE_ROWS = DEC_SEQ * B_HEADS
PAGE_ROWS = PAGE_SIZE * B_HEADS


def _moba_sample_kernel(pt_ref, q_ref, kn_ref, vn_ref, bprev_ref, bnew_ref, far_ref, *refs):
    del pt_ref
    k_pages = refs[:N_PAGES]
    v_pages = refs[N_PAGES:2 * N_PAGES]
    o_ref = refs[2 * N_PAGES]
    s_scr = refs[2 * N_PAGES + 1]
    scale = B_HEAD_DIM ** -0.5
    pages_per_block = MOBA_BLOCK // PAGE_SIZE
    n_past = PAST_LEN // MOBA_BLOCK

    q = q_ref[...]
    qb = q.astype(BF16)
    q3 = q.reshape(DEC_SEQ, B_HEADS, B_HEAD_DIM)
    gate_cols = []
    for n in range(n_past):
        ksum = sum(k_pages[n * pages_per_block + j][...].reshape(PAGE_SIZE, B_HEADS, B_HEAD_DIM).sum(axis=0)
                   for j in range(pages_per_block))
        kmean = ksum * (1.0 / MOBA_BLOCK)
        gate_cols.append(jnp.sum(q3 * kmean[None], axis=-1, keepdims=True).reshape(SAMPLE_ROWS, 1))
    gate = jnp.concatenate(gate_cols, axis=1)
    penalty = jnp.where(_select_topk(gate, n_past) > 0.0, 0.0, NEG_INF)

    def same_head_penalty(cols):
        r = lax.broadcasted_iota(jnp.int32, (SAMPLE_ROWS, cols), 0) % B_HEADS
        c = lax.broadcasted_iota(jnp.int32, (SAMPLE_ROWS, cols), 1) % B_HEADS
        return jnp.where(r == c, 0.0, NEG_INF)

    head_pen = same_head_penalty(PAGE_ROWS)
    far_bias = far_ref[...]
    m_el = jnp.full((SAMPLE_ROWS, PAGE_ROWS), NEG_INF, F32)
    for p in range(N_PAGES):
        n = p // pages_per_block
        s = _dot_nt(qb, k_pages[p][...].astype(BF16)) * scale
        if n == n_past - 1:
            j = p - n * pages_per_block
            s = s + bprev_ref[:, j * PAGE_ROWS:(j + 1) * PAGE_ROWS] + penalty[:, n:n + 1]
        else:
            s = s + (far_bias + penalty[:, n:n + 1])
        s = s + head_pen
        s_scr[p] = s
        m_el = jnp.maximum(m_el, s)

    s_new = _dot_nt(q, kn_ref[...]) * scale + bnew_ref[...] + same_head_penalty(SAMPLE_ROWS)
    t_q = lax.broadcasted_iota(jnp.int32, (SAMPLE_ROWS, SAMPLE_ROWS), 0) // B_HEADS
    t_k = lax.broadcasted_iota(jnp.int32, (SAMPLE_ROWS, SAMPLE_ROWS), 1) // B_HEADS
    s_new = jnp.where(t_k <= t_q, s_new, NEG_INF)

    m = jnp.maximum(jnp.max(m_el, axis=-1, keepdims=True), jnp.max(s_new, axis=-1, keepdims=True))
    p_new = jnp.exp(s_new - m)
    acc = _dot(p_new, vn_ref[...])
    l_el = jnp.zeros((SAMPLE_ROWS, PAGE_ROWS), F32)
    for p in range(N_PAGES):
        pr = jnp.exp(s_scr[p] - m)
        l_el = l_el + pr
        acc = acc + _dot(pr.astype(BF16), v_pages[p][...].astype(BF16))
    l = jnp.sum(p_new, axis=-1, keepdims=True) + jnp.sum(l_el, axis=-1, keepdims=True)
    o_ref[...] = acc / l


def _moba_sample(q_rows, k_rows, v_rows, cache_k_rows, cache_v_rows, layer, page_table_flat,
                 bias_prev, bias_new, far_col):
    n_seq = q_rows.shape[0] // SAMPLE_ROWS
    tok_spec = pl.BlockSpec((SAMPLE_ROWS, B_HEAD_DIM), lambda b, pt: (b, 0))

    def page_spec(p):
        return pl.BlockSpec((None, None, PAGE_ROWS, B_HEAD_DIM),
                            lambda b, pt: (layer, pt[b * N_PAGES + p], 0, 0))

    def const_spec(a):
        return pl.BlockSpec(a.shape, lambda b, pt: (0, 0))

    pages = [page_spec(p) for p in range(N_PAGES)]
    grid_spec = pltpu.PrefetchScalarGridSpec(
        num_scalar_prefetch=1,
        grid=(n_seq,),
        in_specs=[tok_spec, tok_spec, tok_spec, const_spec(bias_prev), const_spec(bias_new),
                  const_spec(far_col)] + pages + pages,
        out_specs=tok_spec,
        scratch_shapes=[pltpu.VMEM((N_PAGES, SAMPLE_ROWS, PAGE_ROWS), F32)],
    )
    return pl.pallas_call(
        _moba_sample_kernel,
        grid_spec=grid_spec,
        out_shape=jax.ShapeDtypeStruct((n_seq * SAMPLE_ROWS, B_HEAD_DIM), F32),
        compiler_params=_cparams("parallel"),
        name="moba_sample",
    )(page_table_flat, q_rows, k_rows, v_rows, bias_prev, bias_new, far_col,
      *([cache_k_rows] * N_PAGES), *([cache_v_rows] * N_PAGES))


def kernel(x_prompt, x_sample, cache_k, cache_v, state_conv, state_h, cache_mem_k, cache_mem_v, page_table,
           mem_prompt, norm_mix, norm_mlp, norm_mem, norm_final, w_in_a, w_out_a, conv_w, conv_b, gate_a_w,
           gate_a_b, gate_x_w, gate_x_b, lru_lambda, w_in_b, w_out_b, rel_bias, w_mem_kv, w_up, w_down):
    n_prompt = BATCH * SEQ
    n_sample = DEC_BATCH * DEC_SEQ
    xp = x_prompt.reshape(n_prompt, D_MODEL)
    xs = x_sample.reshape(n_sample, D_MODEL)

    w_in_a_bf, w_in_b_bf = w_in_a.astype(BF16), w_in_b.astype(BF16)
    w_out_a_bf, w_out_b_bf = w_out_a.astype(BF16), w_out_b.astype(BF16)
    w_up_bf, w_down_bf = w_up.astype(BF16), w_down.astype(BF16)
    wcat_bf = jnp.concatenate([gate_a_w, gate_x_w], axis=-1).astype(BF16)
    gf = norm_final.reshape(1, D_MODEL)

    memk_rows, memv_rows, memk, memv = _mem_kv(mem_prompt.reshape(BATCH * MEM_LEN, D_MODEL), norm_mem.reshape(DEPTH, 1, D_MODEL),
                         w_mem_kv.astype(BF16))
    cmk = cache_mem_k.reshape(DEPTH, DEC_BATCH, XMEM_ROWS, X_HEAD_DIM)
    cmv = cache_mem_v.reshape(DEPTH, DEC_BATCH, XMEM_ROWS, X_HEAD_DIM)
    cache_k_rows = cache_k.reshape(cache_k.shape[0], cache_k.shape[1], PAGE_ROWS, B_HEAD_DIM)
    cache_v_rows = cache_v.reshape(cache_v.shape[0], cache_v.shape[1], PAGE_ROWS, B_HEAD_DIM)
    pt_flat = page_table.reshape(DEC_BATCH * N_PAGES)

    tiles = _bias_tiles(rel_bias)
    bias_prev = jnp.transpose(tiles[:, 1, :, :DEC_SEQ], (2, 1, 0)).reshape(DEC_SEQ, 1, MOBA_BLOCK * B_HEADS)
    bias_prev = jnp.broadcast_to(bias_prev, (DEC_SEQ, B_HEADS, MOBA_BLOCK * B_HEADS))
    bias_prev = bias_prev.reshape(SAMPLE_ROWS, MOBA_BLOCK * B_HEADS)
    bias_new = jnp.transpose(tiles[:, 0, :DEC_SEQ, :DEC_SEQ], (2, 1, 0)).reshape(DEC_SEQ, 1, SAMPLE_ROWS)
    bias_new = jnp.broadcast_to(bias_new, (DEC_SEQ, B_HEADS, SAMPLE_ROWS)).reshape(SAMPLE_ROWS, SAMPLE_ROWS)
    far_col = jnp.tile(rel_bias[REL_BUCKETS - 1], DEC_SEQ).reshape(SAMPLE_ROWS, 1)

    kv_prompt = ()
    ks_l, vs_l = [], []
    cp_l, hp_l, cs_l, hs_l = [], [], [], []
    for i in range(DEPTH):
        j = i // 2
        g_mix = norm_mix[i].reshape(1, D_MODEL)
        g_mlp = norm_mlp[i].reshape(1, D_MODEL)
        final = i == DEPTH - 1
        if i % 2 == 0:
            lru_w = (wcat_bf[j], conv_w[j], conv_b[j].reshape(1, W_A), gate_a_b[j].reshape(1, W_A),
                     gate_x_b[j].reshape(1, W_A), lru_lambda[j].reshape(1, W_A))
            w_out = w_out_a_bf

            def outs_a(qx_kind, qx_dtype):
                return ((0, W_A, "tok", F32), (W_A, W_A, "tok", F32), (2 * W_A, W_X, qx_kind, qx_dtype))

            u, gate, qx = _norm_proj(xp, g_mix, w_in_a_bf, j, outs_a("tok", BF16))
            y_p, utail, htail = _lru_prompt(u, gate, *lru_w)
            cp_l.append(utail[:, SUBLANES - (CONV_WIDTH - 1):, :])
            hp_l.append(htail[:, SUBLANES - 1, :])

            u, gate, qx_s = _norm_proj(xs, g_mix, w_in_a_bf, j, outs_a("rows", F32))
            prev = jnp.pad(state_conv[j], ((0, 0), (SUBLANES - (CONV_WIDTH - 1), 0), (0, 0)))
            h0pad = jnp.pad(state_h[j][:, None, :], ((0, 0), (0, DEC_SEQ - 1), (0, 0)))
            y_s, h_s = _lru_sample(u, gate, prev.reshape(n_sample, W_A), h0pad.reshape(n_sample, W_A), *lru_w)
            cs_l.append(u.reshape(DEC_BATCH, DEC_SEQ, W_A)[:, DEC_SEQ - (CONV_WIDTH - 1):, :])
            hs_l.append(h_s.reshape(DEC_BATCH, DEC_SEQ, W_A)[:, DEC_SEQ - 1, :])
        else:
            w_out = w_out_b_bf

            rows_kind = "rows2" if kv_prompt else "rows"
            outs_p = ((0, W_B, "tok", F32),
                      (W_B, W_B, rows_kind, F32), (W_B, W_B, "tok", BF16), (W_B, W_B, "blockmean", F32),
                      (2 * W_B, W_B, rows_kind, F32), (2 * W_B, W_B, "tok", BF16),
                      (3 * W_B, W_X, "tok", BF16))
            q, k_rows, k_bf, kmean, v_rows, v_bf, qx = _norm_proj(xp, g_mix, w_in_b_bf, j, outs_p, kv_prompt)
            y_p = _moba_prompt(q, k_bf, v_bf, kmean.reshape(BATCH, N_BLOCKS, W_B), tiles, rel_bias)
            kv_prompt = (k_rows, v_rows)

            outs_s = ((0, W_B, "rows", F32), (W_B, W_B, "rows", F32), (2 * W_B, W_B, "rows", F32),
                      (3 * W_B, W_X, "rows", F32))
            q_rows, k_rows, v_rows, qx_s = _norm_proj(xs, g_mix, w_in_b_bf, j, outs_s)
            y_s = _moba_sample(q_rows, k_rows, v_rows, cache_k_rows, cache_v_rows, j, pt_flat,
                               bias_prev, bias_new, far_col).reshape(n_sample, W_B)
            ks_l.append(k_rows.reshape(DEC_BATCH, DEC_SEQ, B_HEADS, B_HEAD_DIM))
            vs_l.append(v_rows.reshape(DEC_BATCH, DEC_SEQ, B_HEADS, B_HEAD_DIM))

        ca_p = _xattn_prompt(qx, memk, memv, i)
        ca_s = _xattn_sample(qx_s, cmk, cmv, i).reshape(n_sample, W_X)
        post_w = (w_out, j, g_mlp, w_up_bf, w_down_bf, i, gf, final)
        xp = _post(xp, y_p, ca_p, *post_w)
        xs = _post(xs, y_s, ca_s, *post_w)

    mem_shape = (DEPTH, BATCH, MEM_LEN, X_HEADS, X_HEAD_DIM)
    kv_shape = (DEPTH // 2, BATCH, SEQ, B_HEADS, B_HEAD_DIM)
    return (xp.reshape(BATCH, SEQ, D_MODEL), xs.reshape(DEC_BATCH, DEC_SEQ, D_MODEL),
            kv_prompt[0].reshape(kv_shape), kv_prompt[1].reshape(kv_shape), jnp.stack(ks_l), jnp.stack(vs_l),
            jnp.stack(cp_l), jnp.stack(hp_l), jnp.stack(cs_l), jnp.stack(hs_l),
            memk_rows.reshape(mem_shape), memv_rows.reshape(mem_shape))
```

```python
import functools
import math

import jax
import jax.numpy as jnp
from jax import lax
from jax.experimental import pallas as pl
from jax.experimental.pallas import tpu as pltpu

F32 = jnp.float32
BF16 = jnp.bfloat16

D_MODEL = 1024
BATCH = 8
SEQ = 2048
DEPTH = 4
DEC_BATCH = 128
DEC_SEQ = 8
PAST_LEN = 2048
PAGE_SIZE = 128
N_PAGES = PAST_LEN // PAGE_SIZE
W_A = D_MODEL
LRU_BLOCKS = 8
LRU_BLOCK_DIM = W_A // LRU_BLOCKS
CONV_WIDTH = 4
LRU_C = 8.0
B_HEADS = 8
B_HEAD_DIM = D_MODEL // B_HEADS
W_B = B_HEADS * B_HEAD_DIM
MOBA_BLOCK = 256
MOBA_TOPK = 3
N_BLOCKS = SEQ // MOBA_BLOCK
REL_BUCKETS = 32
REL_MAX_DIST = 128
X_HEADS = 4
X_HEAD_DIM = 128
W_X = X_HEADS * X_HEAD_DIM
MEM_LEN = 256
D_FF = 4 * D_MODEL
EPS = 1e-6

SUBLANES = 8
VMEM_LIMIT = 56 * 1024 * 1024
NEG_INF = float("-inf")


def _cparams(*sem):
    return pltpu.CompilerParams(dimension_semantics=sem, vmem_limit_bytes=VMEM_LIMIT)


def _dot(a, b):
    return jnp.dot(a, b, preferred_element_type=F32)


def _dot_nt(a, b, precision=None):
    return lax.dot_general(a, b, (((1,), (1,)), ((), ())), preferred_element_type=F32, precision=precision)


def _rms(x, g):
    return x * lax.rsqrt(jnp.mean(x * x, axis=-1, keepdims=True) + EPS) * g


def _t5_thresholds():
    max_exact = REL_BUCKETS // 2
    thr = []
    for k in range(1, REL_BUCKETS - max_exact):
        d = max_exact
        while int(math.log(d / max_exact) / math.log(REL_MAX_DIST / max_exact) * (REL_BUCKETS - max_exact)) < k:
            d += 1
        thr.append(d)
    return thr


T5_THRESHOLDS = _t5_thresholds()


def _mem_kv_kernel(x_ref, g_ref, w_ref, k_rows_ref, v_rows_ref, k_bf_ref, v_bf_ref):
    tm = x_ref.shape[0]
    hn = _rms(x_ref[...], g_ref[...]).astype(BF16)
    r = _dot(hn, w_ref[...])
    for rows_ref, bf_ref, off in ((k_rows_ref, k_bf_ref, 0), (v_rows_ref, v_bf_ref, W_X)):
        bf_ref[...] = r[:, off:off + W_X].astype(BF16)
        for h in range(X_HEADS):
            rows_ref[pl.ds(h, tm, stride=X_HEADS), :] = r[:, off + h * X_HEAD_DIM:off + (h + 1) * X_HEAD_DIM]


def _mem_kv(mem2d, norm_mem, w_bf):
    m = mem2d.shape[0]
    tm = 512
    rows_spec = pl.BlockSpec((None, tm * X_HEADS, X_HEAD_DIM), lambda l, i: (l, i, 0))
    tok_spec = pl.BlockSpec((None, tm, W_X), lambda l, i: (l, i, 0))
    rows_shape = jax.ShapeDtypeStruct((DEPTH, m * X_HEADS, X_HEAD_DIM), F32)
    tok_shape = jax.ShapeDtypeStruct((DEPTH, m, W_X), BF16)
    return pl.pallas_call(
        _mem_kv_kernel,
        grid=(DEPTH, m // tm),
        in_specs=[
            pl.BlockSpec((tm, D_MODEL), lambda l, i: (i, 0)),
            pl.BlockSpec((None, 1, D_MODEL), lambda l, i: (l, 0, 0)),
            pl.BlockSpec((None, D_MODEL, 2 * W_X), lambda l, i: (l, 0, 0)),
        ],
        out_specs=[rows_spec, rows_spec, tok_spec, tok_spec],
        out_shape=[rows_shape, rows_shape, tok_shape, tok_shape],
        compiler_params=_cparams("parallel", "parallel"),
        name="mem_kv",
    )(mem2d, norm_mem, w_bf)


LANES = 128


def _norm_proj_kernel(x_ref, g_ref, w_ref, *refs, outs):
    n_earlier = sum(kind == "rows2" for _, _, kind, _ in outs)
    earlier_refs = list(refs[:n_earlier])
    out_refs = refs[n_earlier:]
    tm = x_ref.shape[0]
    for c in range(tm // MOBA_BLOCK):
        rows = slice(c * MOBA_BLOCK, (c + 1) * MOBA_BLOCK)
        hn = _rms(x_ref[rows, :], g_ref[...]).astype(BF16)
        results = {}
        earlier = iter(earlier_refs)
        for o_ref, (off, n, kind, _) in zip(out_refs, outs):
            if off not in results:
                results[off] = _dot(hn, w_ref[:, off:off + n])
            r = results[off]
            heads = n // LANES
            head_rows = slice(c * MOBA_BLOCK * heads, (c + 1) * MOBA_BLOCK * heads)
            if kind == "tok":
                o_ref[rows, :] = r.astype(o_ref.dtype)
            elif kind == "rows":
                for h in range(heads):
                    o_ref[pl.ds(head_rows.start + h, MOBA_BLOCK, stride=heads), :] = r[:, h * LANES:(h + 1) * LANES]
            elif kind == "rows2":
                o_ref[0, head_rows, :] = next(earlier)[head_rows, :]
                for h in range(heads):
                    o_ref[1, pl.ds(head_rows.start + h, MOBA_BLOCK, stride=heads), :] = (
                        r[:, h * LANES:(h + 1) * LANES])
            else:
                o_ref[c] = jnp.mean(r, axis=0, keepdims=True)


def _norm_proj(x, g, w_bf, mixer_layer, outs, earlier=()):
    m = x.shape[0]
    tm = 512
    n_all = w_bf.shape[2]
    specs, shapes, earlier_specs = [], [], []
    for _, n, kind, dt in outs:
        heads = n // LANES
        if kind == "tok":
            specs.append(pl.BlockSpec((tm, n), lambda i: (i, 0)))
            shapes.append(jax.ShapeDtypeStruct((m, n), dt))
        elif kind == "rows":
            specs.append(pl.BlockSpec((tm * heads, LANES), lambda i: (i, 0)))
            shapes.append(jax.ShapeDtypeStruct((m * heads, LANES), dt))
        elif kind == "rows2":
            earlier_specs.append(pl.BlockSpec((tm * heads, LANES), lambda i: (i, 0)))
            specs.append(pl.BlockSpec((2, tm * heads, LANES), lambda i: (0, i, 0)))
            shapes.append(jax.ShapeDtypeStruct((2, m * heads, LANES), dt))
        else:
            specs.append(pl.BlockSpec((tm // MOBA_BLOCK, 1, n), lambda i: (i, 0, 0)))
            shapes.append(jax.ShapeDtypeStruct((m // MOBA_BLOCK, 1, n), dt))
    return pl.pallas_call(
        functools.partial(_norm_proj_kernel, outs=outs),
        grid=(m // tm,),
        in_specs=[
            pl.BlockSpec((tm, D_MODEL), lambda i: (i, 0)),
            pl.BlockSpec((1, D_MODEL), lambda i: (0, 0)),
            pl.BlockSpec((None, D_MODEL, n_all), lambda i: (mixer_layer, 0, 0), pipeline_mode=pl.Buffered(1)),
        ] + earlier_specs,
        out_specs=specs,
        out_shape=shapes,
        compiler_params=_cparams("parallel"),
        name="norm_proj",
    )(x, g, w_bf, *earlier)


def _softmax_pv(s, v_bf):
    m = jnp.max(s, axis=-1, keepdims=True)
    p = jnp.exp(s - m)
    l = jnp.sum(p, axis=-1, keepdims=True)
    return _dot(p.astype(BF16), v_bf) / l


XQ_ROWS = DEC_SEQ * X_HEADS
XMEM_ROWS = MEM_LEN * X_HEADS


def _xattn_sample_kernel(q_ref, k_ref, v_ref, o_ref, *, group):
    scale = X_HEAD_DIM ** -0.5
    r = lax.broadcasted_iota(jnp.int32, (XQ_ROWS, XMEM_ROWS), 0) % X_HEADS
    c = lax.broadcasted_iota(jnp.int32, (XQ_ROWS, XMEM_ROWS), 1) % X_HEADS
    head_pen = jnp.where(r == c, 0.0, NEG_INF)
    for g in range(group):
        rows = slice(g * XQ_ROWS, (g + 1) * XQ_ROWS)
        s = _dot_nt(q_ref[rows, :].astype(BF16), k_ref[g].astype(BF16)) * scale + head_pen
        o_ref[rows, :] = _softmax_pv(s, v_ref[g].astype(BF16))


def _xattn_sample(q_rows, cmk_rows, cmv_rows, layer):
    group = 8
    n_seq = q_rows.shape[0] // XQ_ROWS
    kv_spec = pl.BlockSpec((None, group, XMEM_ROWS, X_HEAD_DIM), lambda i: (layer, i, 0, 0))
    q_spec = pl.BlockSpec((group * XQ_ROWS, X_HEAD_DIM), lambda i: (i, 0))
    return pl.pallas_call(
        functools.partial(_xattn_sample_kernel, group=group),
        grid=(n_seq // group,),
        in_specs=[q_spec, kv_spec, kv_spec],
        out_specs=q_spec,
        out_shape=jax.ShapeDtypeStruct((n_seq * XQ_ROWS, X_HEAD_DIM), F32),
        compiler_params=_cparams("parallel"),
        name="xattn_sample",
    )(q_rows, cmk_rows, cmv_rows)


POST_ROW_CHUNK = 256


def _post_kernel(x_ref, y_ref, *refs, final_norm, fused_xattn):
    if fused_xattn:
        q_ref, mk_ref, mv_ref = refs[:3]
        refs = refs[3:]
    else:
        c_ref = refs[0]
        refs = refs[1:]
    wy_ref, wc_ref, g_ref, wu_ref, wd_ref, gf_ref, o_ref, hn_scr = refs[:8]
    ca_scr = refs[8] if fused_xattn else None
    f = pl.program_id(1)
    tm = o_ref.shape[0]

    def cross_attention(rows):
        return (ca_scr if fused_xattn else c_ref)[rows, :].astype(BF16)

    @pl.when(f == 0)
    def _():
        if fused_xattn:
            for h in range(X_HEADS):
                sl = slice(h * X_HEAD_DIM, (h + 1) * X_HEAD_DIM)
                s = _dot_nt(q_ref[:, sl], mk_ref[:, sl]) * (X_HEAD_DIM ** -0.5)
                ca_scr[:, sl] = _softmax_pv(s, mv_ref[:, sl]).astype(BF16)
        for c in range(tm // POST_ROW_CHUNK):
            rows = slice(c * POST_ROW_CHUNK, (c + 1) * POST_ROW_CHUNK)
            xn = (x_ref[rows, :] + _dot(y_ref[rows, :].astype(BF16), wy_ref[...])
                  + _dot(cross_attention(rows), wc_ref[...]))
            o_ref[rows, :] = xn
            hn_scr[rows, :] = _rms(xn, g_ref[...]).astype(BF16)

    h = _dot(hn_scr[...], wu_ref[...])
    h = jnp.square(jnp.maximum(h, 0.0)).astype(BF16)
    o_ref[...] += _dot(h, wd_ref[...])

    if final_norm:
        @pl.when(f == pl.num_programs(1) - 1)
        def _():
            o_ref[...] = _rms(o_ref[...], gf_ref[...])


def _post(x, y, ca, w_out, mixer_layer, g, w_up, w_down, layer, gf, final_norm):
    m = x.shape[0]
    tm, tf = 1024, 1024
    mix_w = y.shape[1]
    row = lambda i, f: (i, 0)
    const = lambda i, f: (0, 0)
    fused_xattn = isinstance(ca, tuple)
    if fused_xattn:
        tiles_per_seq = SEQ // tm
        mem_spec = pl.BlockSpec((None, MEM_LEN, W_X), lambda i, f: (layer, i // tiles_per_seq, 0))
        ca_args = ca
        ca_specs = [pl.BlockSpec((tm, W_X), row), mem_spec, mem_spec]
    else:
        ca_args = (ca,)
        ca_specs = [pl.BlockSpec((tm, W_X), row)]
    return pl.pallas_call(
        functools.partial(_post_kernel, final_norm=final_norm, fused_xattn=fused_xattn),
        grid=(m // tm, D_FF // tf),
        in_specs=[
            pl.BlockSpec((tm, D_MODEL), row),
            pl.BlockSpec((tm, mix_w), row),
        ] + ca_specs + [
            pl.BlockSpec((None, mix_w, D_MODEL), lambda i, f: (mixer_layer, 0, 0)),
            pl.BlockSpec((None, W_X, D_MODEL), lambda i, f: (mixer_layer, mix_w // W_X, 0)),
            pl.BlockSpec((1, D_MODEL), const),
            pl.BlockSpec((None, D_MODEL, tf), lambda i, f: (layer, 0, f)),
            pl.BlockSpec((None, tf, D_MODEL), lambda i, f: (layer, f, 0)),
            pl.BlockSpec((1, D_MODEL), const),
        ],
        out_specs=pl.BlockSpec((tm, D_MODEL), row),
        out_shape=jax.ShapeDtypeStruct((m, D_MODEL), F32),
        scratch_shapes=[pltpu.VMEM((tm, D_MODEL), BF16)] + ([pltpu.VMEM((tm, W_X), BF16)] if fused_xattn else []),
        compiler_params=_cparams("parallel", "arbitrary"),
        name="post",
    )(x, y, *ca_args, w_out, w_out, g, w_up, w_down, gf)


def _gelu_tanh(x):
    return x * (0.5 * (1.0 + jnp.tanh(math.sqrt(2.0 / math.pi) * (x + 0.044715 * (x * x * x)))))


def _conv_rows(u, u_prev, cw_ref, cb_ref):
    rows = u.shape[0]
    g = rows // SUBLANES
    u3 = u.reshape(g, SUBLANES, W_A)
    p3 = u_prev.reshape(g, SUBLANES, W_A)
    sub = lax.broadcasted_iota(jnp.int32, (g, SUBLANES, W_A), 1)
    xc = cb_ref[...] + cw_ref[CONV_WIDTH - 1:CONV_WIDTH, :] * u3
    for j in range(1, CONV_WIDTH):
        sh = jnp.where(sub >= j, pltpu.roll(u3, j, axis=1), pltpu.roll(p3, j, axis=1))
        xc = xc + cw_ref[CONV_WIDTH - 1 - j:CONV_WIDTH - j, :] * sh
    return xc.reshape(rows, W_A)


def _lru_coeffs(xc, wcat_ref, gab_ref, gxb_ref, lam_ref, first_row, a_scr, b_scr):
    z = -lam_ref[...]
    softplus = jnp.maximum(z, 0.0) + jnp.log1p(jnp.exp(-jnp.abs(z)))
    for n in range(LRU_BLOCKS):
        sl = slice(n * LRU_BLOCK_DIM, (n + 1) * LRU_BLOCK_DIM)
        xn = xc[:, sl]
        ra = _dot(xn.astype(BF16), wcat_ref[n])
        r = jax.nn.sigmoid(ra[:, :LRU_BLOCK_DIM] + gab_ref[:, sl])
        ig = jax.nn.sigmoid(ra[:, LRU_BLOCK_DIM:] + gxb_ref[:, sl])
        log_a = -LRU_C * r * softplus[:, sl]
        th = jnp.tanh(log_a)
        mult = jnp.sqrt(-2.0 * th / (1.0 - th))
        if first_row is not None:
            mult = jnp.where(first_row, 1.0, mult)
        a_scr[:, sl] = jnp.exp(log_a)
        b_scr[:, sl] = mult * ig * xn


def _scan_within_tiles(a, b):
    rows = a.shape[0]
    g = rows // SUBLANES
    a3 = a.reshape(g, SUBLANES, W_A)
    b3 = b.reshape(g, SUBLANES, W_A)
    sub = lax.broadcasted_iota(jnp.int32, (g, SUBLANES, W_A), 1)
    for s in (1, 2, 4):
        keep = sub >= s
        b3 = jnp.where(keep, a3 * pltpu.roll(b3, s, axis=1) + b3, b3)
        a3 = jnp.where(keep, a3 * pltpu.roll(a3, s, axis=1), a3)
    return a3.reshape(rows, W_A), b3.reshape(rows, W_A)


def _lru_prompt_kernel(u_ref, gate_ref, wcat_ref, cw_ref, cb_ref, gab_ref, gxb_ref, lam_ref,
                       y_ref, utail_ref, htail_ref, tail_scr, carry_scr, a_scr, b_scr, h_scr, *, tile):
    t = pl.program_id(1)

    @pl.when(t == 0)
    def _():
        tail_scr[...] = jnp.zeros_like(tail_scr)
        carry_scr[...] = jnp.zeros_like(carry_scr)

    u = u_ref[...]
    u_prev = jnp.concatenate([tail_scr[...], u[:tile - SUBLANES]], axis=0)
    tail_scr[...] = u[tile - SUBLANES:]
    xc = _conv_rows(u, u_prev, cw_ref, cb_ref)
    row = lax.broadcasted_iota(jnp.int32, (tile, LRU_BLOCK_DIM), 0)
    first_row = jnp.logical_and(row == 0, t == 0)
    _lru_coeffs(xc, wcat_ref, gab_ref, gxb_ref, lam_ref, first_row, a_scr, b_scr)
    a, b = _scan_within_tiles(a_scr[...], b_scr[...])
    a_scr[...] = a
    b_scr[...] = b

    def step(g, carry):
        rows = pl.ds(pl.multiple_of(g * SUBLANES, SUBLANES), SUBLANES)
        h = b_scr[rows, :] + a_scr[rows, :] * carry
        h_scr[rows, :] = h
        return jnp.broadcast_to(h[SUBLANES - 1:SUBLANES, :], (SUBLANES, W_A))

    carry_scr[...] = lax.fori_loop(0, tile // SUBLANES, step, carry_scr[...])
    y_ref[...] = (_gelu_tanh(gate_ref[...]) * h_scr[...]).astype(y_ref.dtype)
    utail_ref[...] = u[tile - SUBLANES:]
    htail_ref[...] = h_scr[tile - SUBLANES:, :]


def _lru_prompt(u, gate, wcat, cw, cb, gab, gxb, lam):
    tile = 256
    nt = SEQ // tile
    row = lambda b, t: (b * nt + t, 0)
    const2 = lambda b, t: (0, 0)
    vec = pl.BlockSpec((1, W_A), const2)
    tail_spec = pl.BlockSpec((None, SUBLANES, W_A), lambda b, t: (b, 0, 0))
    return pl.pallas_call(
        functools.partial(_lru_prompt_kernel, tile=tile),
        grid=(BATCH, nt),
        in_specs=[
            pl.BlockSpec((tile, W_A), row),
            pl.BlockSpec((tile, W_A), row),
            pl.BlockSpec(wcat.shape, lambda b, t: (0, 0, 0)),
            pl.BlockSpec((CONV_WIDTH, W_A), const2),
            vec, vec, vec, vec,
        ],
        out_specs=[pl.BlockSpec((tile, W_A), row), tail_spec, tail_spec],
        out_shape=[
            jax.ShapeDtypeStruct((BATCH * SEQ, W_A), BF16),
            jax.ShapeDtypeStruct((BATCH, SUBLANES, W_A), F32),
            jax.ShapeDtypeStruct((BATCH, SUBLANES, W_A), F32),
        ],
        scratch_shapes=[
            pltpu.VMEM((SUBLANES, W_A), F32),
            pltpu.VMEM((SUBLANES, W_A), F32),
            pltpu.VMEM((tile, W_A), F32),
            pltpu.VMEM((tile, W_A), F32),
            pltpu.VMEM((tile, W_A), F32),
        ],
        compiler_params=_cparams("parallel", "arbitrary"),
        name="lru_prompt",
    )(u, gate, wcat, cw, cb, gab, gxb, lam)


def _lru_sample_kernel(u_ref, gate_ref, prev_ref, h0_ref, wcat_ref, cw_ref, cb_ref, gab_ref, gxb_ref, lam_ref,
                       y_ref, h_ref, a_scr, b_scr):
    xc = _conv_rows(u_ref[...], prev_ref[...], cw_ref, cb_ref)
    _lru_coeffs(xc, wcat_ref, gab_ref, gxb_ref, lam_ref, None, a_scr, b_scr)
    a = a_scr[...]
    _, h = _scan_within_tiles(a, b_scr[...] + a * h0_ref[...])
    h_ref[...] = h
    y_ref[...] = (_gelu_tanh(gate_ref[...]) * h).astype(y_ref.dtype)


def _lru_sample(u, gate, prev, h0pad, wcat, cw, cb, gab, gxb, lam):
    m = u.shape[0]
    tile = 256
    row = lambda i: (i, 0)
    const2 = lambda i: (0, 0)
    vec = pl.BlockSpec((1, W_A), const2)
    blk = pl.BlockSpec((tile, W_A), row)
    return pl.pallas_call(
        _lru_sample_kernel,
        grid=(m // tile,),
        in_specs=[blk, blk, blk, blk,
                  pl.BlockSpec(wcat.shape, lambda i: (0, 0, 0)),
                  pl.BlockSpec((CONV_WIDTH, W_A), const2),
                  vec, vec, vec, vec],
        out_specs=[blk, blk],
        out_shape=[jax.ShapeDtypeStruct((m, W_A), BF16), jax.ShapeDtypeStruct((m, W_A), F32)],
        scratch_shapes=[pltpu.VMEM((tile, W_A), F32), pltpu.VMEM((tile, W_A), F32)],
        compiler_params=_cparams("parallel"),
        name="lru_sample",
    )(u, gate, prev, h0pad, wcat, cw, cb, gab, gxb, lam)


def _bias_tile_kernel(rel_ref, o_ref):
    h = pl.program_id(0)
    w = pl.program_id(1)
    shape = (MOBA_BLOCK, MOBA_BLOCK)
    r = lax.broadcasted_iota(jnp.int32, shape, 0)
    c = lax.broadcasted_iota(jnp.int32, shape, 1)
    dist = jnp.maximum(w * MOBA_BLOCK + c - r, 0)
    large = jnp.full(shape, REL_BUCKETS // 2, jnp.int32)
    for thr in T5_THRESHOLDS:
        large = large + (dist >= thr).astype(jnp.int32)
    bucket = jnp.where(dist < REL_BUCKETS // 2, dist, large)
    out = jnp.zeros(shape, F32)
    for b in range(REL_BUCKETS):
        out = jnp.where(bucket == b, rel_ref[b, h], out)
    o_ref[...] = out


def _bias_tiles(rel_bias):
    return pl.pallas_call(
        _bias_tile_kernel,
        grid=(B_HEADS, 2),
        in_specs=[pl.BlockSpec(memory_space=pltpu.SMEM)],
        out_specs=pl.BlockSpec((None, None, MOBA_BLOCK, MOBA_BLOCK), lambda h, w: (h, w, 0, 0)),
        out_shape=jax.ShapeDtypeStruct((B_HEADS, 2, MOBA_BLOCK, MOBA_BLOCK), F32),
        compiler_params=_cparams("parallel", "parallel"),
        name="bias_tiles",
    )(rel_bias)


def _select_topk(gate, n_past):
    blk = lax.broadcasted_iota(jnp.int32, gate.shape, 1)
    past = blk < n_past
    g = jnp.where(past, gate, NEG_INF)
    rank = jnp.zeros(gate.shape, jnp.int32)
    for m in range(gate.shape[1]):
        gm = g[:, m:m + 1]
        beats = jnp.logical_or(gm > g, jnp.logical_and(gm == g, blk > m))
        rank = rank + beats.astype(jnp.int32)
    return jnp.where(jnp.logical_and(past, rank < MOBA_TOPK), 1.0, 0.0)


def _select_topk_t(gate_t, n_past):
    blk = lax.broadcasted_iota(jnp.int32, gate_t.shape, 0)
    past = blk < n_past
    g = jnp.where(past, gate_t, NEG_INF)
    rank = jnp.zeros(gate_t.shape, jnp.int32)
    for m in range(gate_t.shape[0]):
        gm = g[m:m + 1, :]
        beats = jnp.logical_or(gm > g, jnp.logical_and(gm == g, blk > m))
        rank = rank + beats.astype(jnp.int32)
    return jnp.where(jnp.logical_and(past, rank < MOBA_TOPK), 1.0, 0.0)


LOG2E = math.log2(math.e)


def _moba_prompt_kernel(rel_ref, q_ref, k_ref, v_ref, kmean_ref, bias_ref, o_ref, vt_scr, s_scr, bias_scr):
    h = pl.program_id(1)
    scale = B_HEAD_DIM ** -0.5

    for n in range(N_BLOCKS):
        cols = slice(n * MOBA_BLOCK, (n + 1) * MOBA_BLOCK)
        vt_scr[:, cols] = v_ref[cols, :].astype(F32).T.astype(BF16)

    gate_all = _dot_nt(kmean_ref[...], q_ref[...], precision=lax.Precision.HIGHEST)
    query_block = lax.broadcasted_iota(jnp.int32, gate_all.shape, 1) // MOBA_BLOCK
    penalty_all = jnp.where(_select_topk_t(gate_all, query_block) > 0.0, 0.0, NEG_INF)
    far_bias = rel_ref[REL_BUCKETS - 1, h] * LOG2E
    key = lax.broadcasted_iota(jnp.int32, (MOBA_BLOCK, MOBA_BLOCK), 0)
    query = lax.broadcasted_iota(jnp.int32, (MOBA_BLOCK, MOBA_BLOCK), 1)
    bias_scr[0] = jnp.where(key <= query, bias_ref[0] * LOG2E, NEG_INF)
    bias_scr[1] = bias_ref[1] * LOG2E

    def tile_rows(x, op):
        return op(x.reshape(MOBA_BLOCK // SUBLANES, SUBLANES, MOBA_BLOCK), axis=0)

    def attend(own):
        q_rows = slice(own * MOBA_BLOCK, (own + 1) * MOBA_BLOCK)
        qb = q_ref[q_rows, :].astype(BF16)
        penalty_t = penalty_all[:, q_rows]
        first_tile = own * (own + 1) // 2
        m_part = None
        for n in range(own, -1, -1):
            rows = slice(n * MOBA_BLOCK, (n + 1) * MOBA_BLOCK)
            s = _dot_nt(k_ref[rows, :], qb) * (scale * LOG2E)
            if n == own:
                s = s + bias_scr[0]
            elif n == own - 1:
                s = s + bias_scr[1] + penalty_t[n:n + 1, :]
            else:
                s = s + (far_bias + penalty_t[n:n + 1, :])
            s_scr[first_tile + n] = s
            part = tile_rows(s, jnp.max)
            m_part = part if m_part is None else jnp.maximum(m_part, part)
        m = jnp.max(m_part, axis=0, keepdims=True)
        l_part = jnp.zeros((SUBLANES, MOBA_BLOCK), F32)
        acc = jnp.zeros((B_HEAD_DIM, MOBA_BLOCK), F32)
        for n in range(own + 1):
            rows = slice(n * MOBA_BLOCK, (n + 1) * MOBA_BLOCK)
            p = jnp.exp2(s_scr[first_tile + n] - m)
            l_part = l_part + tile_rows(p, jnp.sum)
            acc = acc + _dot(vt_scr[:, rows], p.astype(BF16))
        l = jnp.sum(l_part, axis=0, keepdims=True)
        o_ref[q_rows, :] = (acc / l).T.astype(o_ref.dtype)

    for own in range(N_BLOCKS):
        attend(own)


def _moba_prompt(q, k_bf, v_bf, kmean, bias_tiles, rel_bias):
    batch = q.shape[0] // SEQ
    seq_spec = pl.BlockSpec((SEQ, B_HEAD_DIM), lambda b, h: (b, h))
    n_tiles = N_BLOCKS * (N_BLOCKS + 1) // 2
    return pl.pallas_call(
        _moba_prompt_kernel,
        grid=(batch, B_HEADS),
        in_specs=[
            pl.BlockSpec(memory_space=pltpu.SMEM),
            seq_spec, seq_spec, seq_spec,
            pl.BlockSpec((None, N_BLOCKS, B_HEAD_DIM), lambda b, h: (b, 0, h)),
            pl.BlockSpec((None, 2, MOBA_BLOCK, MOBA_BLOCK), lambda b, h: (h, 0, 0, 0)),
        ],
        out_specs=seq_spec,
        out_shape=jax.ShapeDtypeStruct((batch * SEQ, W_B), BF16),
        scratch_shapes=[pltpu.VMEM((B_HEAD_DIM, SEQ), BF16),
                        pltpu.VMEM((n_tiles, MOBA_BLOCK, MOBA_BLOCK), F32),
                        pltpu.VMEM((2, MOBA_BLOCK, MOBA_BLOCK), F32)],
        compiler_params=_cparams("parallel", "parallel"),
        name="moba_prompt",
    )(rel_bias, q, k_bf, v_bf, kmean, bias_tiles)


SAMPLE_ROWS = DEC_SEQ * B_HEADS
PAGE_ROWS = PAGE_SIZE * B_HEADS


def _moba_sample_kernel(pt_ref, q_ref, kn_ref, vn_ref, bprev_ref, bnew_ref, far_ref, *refs):
    del pt_ref
    k_pages = refs[:N_PAGES]
    v_pages = refs[N_PAGES:2 * N_PAGES]
    o_ref = refs[2 * N_PAGES]
    s_scr = refs[2 * N_PAGES + 1]
    scale = B_HEAD_DIM ** -0.5
    pages_per_block = MOBA_BLOCK // PAGE_SIZE
    n_past = PAST_LEN // MOBA_BLOCK

    q = q_ref[...]
    qb = q.astype(BF16)
    q3 = q.reshape(DEC_SEQ, B_HEADS, B_HEAD_DIM)
    gate_cols = []
    for n in range(n_past):
        ksum = sum(k_pages[n * pages_per_block + j][...].reshape(PAGE_SIZE, B_HEADS, B_HEAD_DIM).sum(axis=0)
                   for j in range(pages_per_block))
        kmean = ksum * (1.0 / MOBA_BLOCK)
        gate_cols.append(jnp.sum(q3 * kmean[None], axis=-1, keepdims=True).reshape(SAMPLE_ROWS, 1))
    gate = jnp.concatenate(gate_cols, axis=1)
    penalty = jnp.where(_select_topk(gate, n_past) > 0.0, 0.0, NEG_INF)

    def same_head_penalty(cols):
        r = lax.broadcasted_iota(jnp.int32, (SAMPLE_ROWS, cols), 0) % B_HEADS
        c = lax.broadcasted_iota(jnp.int32, (SAMPLE_ROWS, cols), 1) % B_HEADS
        return jnp.where(r == c, 0.0, NEG_INF)

    head_pen = same_head_penalty(PAGE_ROWS)
    far_bias = far_ref[...]
    m_el = jnp.full((SAMPLE_ROWS, PAGE_ROWS), NEG_INF, F32)
    for p in range(N_PAGES):
        n = p // pages_per_block
        s = _dot_nt(qb, k_pages[p][...].astype(BF16)) * scale
        if n == n_past - 1:
            j = p - n * pages_per_block
            s = s + bprev_ref[:, j * PAGE_ROWS:(j + 1) * PAGE_ROWS] + penalty[:, n:n + 1]
        else:
            s = s + (far_bias + penalty[:, n:n + 1])
        s = s + head_pen
        s_scr[p] = s
        m_el = jnp.maximum(m_el, s)

    s_new = _dot_nt(q, kn_ref[...]) * scale + bnew_ref[...] + same_head_penalty(SAMPLE_ROWS)
    t_q = lax.broadcasted_iota(jnp.int32, (SAMPLE_ROWS, SAMPLE_ROWS), 0) // B_HEADS
    t_k = lax.broadcasted_iota(jnp.int32, (SAMPLE_ROWS, SAMPLE_ROWS), 1) // B_HEADS
    s_new = jnp.where(t_k <= t_q, s_new, NEG_INF)

    m = jnp.maximum(jnp.max(m_el, axis=-1, keepdims=True), jnp.max(s_new, axis=-1, keepdims=True))
    p_new = jnp.exp(s_new - m)
    acc = _dot(p_new, vn_ref[...])
    l_el = jnp.zeros((SAMPLE_ROWS, PAGE_ROWS), F32)
    for p in range(N_PAGES):
        pr = jnp.exp(s_scr[p] - m)
        l_el = l_el + pr
        acc = acc + _dot(pr.astype(BF16), v_pages[p][...].astype(BF16))
    l = jnp.sum(p_new, axis=-1, keepdims=True) + jnp.sum(l_el, axis=-1, keepdims=True)
    o_ref[...] = acc / l


def _moba_sample(q_rows, k_rows, v_rows, cache_k_rows, cache_v_rows, layer, page_table_flat,
                 bias_prev, bias_new, far_col):
    n_seq = q_rows.shape[0] // SAMPLE_ROWS
    tok_spec = pl.BlockSpec((SAMPLE_ROWS, B_HEAD_DIM), lambda b, pt: (b, 0))

    def page_spec(p):
        return pl.BlockSpec((None, None, PAGE_ROWS, B_HEAD_DIM),
                            lambda b, pt: (layer, pt[b * N_PAGES + p], 0, 0))

    def const_spec(a):
        return pl.BlockSpec(a.shape, lambda b, pt: (0, 0))

    pages = [page_spec(p) for p in range(N_PAGES)]
    grid_spec = pltpu.PrefetchScalarGridSpec(
        num_scalar_prefetch=1,
        grid=(n_seq,),
        in_specs=[tok_spec, tok_spec, tok_spec, const_spec(bias_prev), const_spec(bias_new),
                  const_spec(far_col)] + pages + pages,
        out_specs=tok_spec,
        scratch_shapes=[pltpu.VMEM((N_PAGES, SAMPLE_ROWS, PAGE_ROWS), F32)],
    )
    return pl.pallas_call(
        _moba_sample_kernel,
        grid_spec=grid_spec,
        out_shape=jax.ShapeDtypeStruct((n_seq * SAMPLE_ROWS, B_HEAD_DIM), F32),
        compiler_params=_cparams("parallel"),
        name="moba_sample",
    )(page_table_flat, q_rows, k_rows, v_rows, bias_prev, bias_new, far_col,
      *([cache_k_rows] * N_PAGES), *([cache_v_rows] * N_PAGES))


def kernel(x_prompt, x_sample, cache_k, cache_v, state_conv, state_h, cache_mem_k, cache_mem_v, page_table,
           mem_prompt, norm_mix, norm_mlp, norm_mem, norm_final, w_in_a, w_out_a, conv_w, conv_b, gate_a_w,
           gate_a_b, gate_x_w, gate_x_b, lru_lambda, w_in_b, w_out_b, rel_bias, w_mem_kv, w_up, w_down):
    n_prompt = BATCH * SEQ
    n_sample = DEC_BATCH * DEC_SEQ
    xp = x_prompt.reshape(n_prompt, D_MODEL)
    xs = x_sample.reshape(n_sample, D_MODEL)

    w_in_a_bf, w_in_b_bf = w_in_a.astype(BF16), w_in_b.astype(BF16)
    w_out_a_bf, w_out_b_bf = w_out_a.astype(BF16), w_out_b.astype(BF16)
    w_up_bf, w_down_bf = w_up.astype(BF16), w_down.astype(BF16)
    wcat_bf = jnp.concatenate([gate_a_w, gate_x_w], axis=-1).astype(BF16)
    gf = norm_final.reshape(1, D_MODEL)

    memk_rows, memv_rows, memk, memv = _mem_kv(mem_prompt.reshape(BATCH * MEM_LEN, D_MODEL), norm_mem.reshape(DEPTH, 1, D_MODEL),
                         w_mem_kv.astype(BF16))
    cmk = cache_mem_k.reshape(DEPTH, DEC_BATCH, XMEM_ROWS, X_HEAD_DIM)
    cmv = cache_mem_v.reshape(DEPTH, DEC_BATCH, XMEM_ROWS, X_HEAD_DIM)
    cache_k_rows = cache_k.reshape(cache_k.shape[0], cache_k.shape[1], PAGE_ROWS, B_HEAD_DIM)
    cache_v_rows = cache_v.reshape(cache_v.shape[0], cache_v.shape[1], PAGE_ROWS, B_HEAD_DIM)
    pt_flat = page_table.reshape(DEC_BATCH * N_PAGES)

    tiles = _bias_tiles(rel_bias)
    bias_prev = jnp.transpose(tiles[:, 1, :, :DEC_SEQ], (2, 1, 0)).reshape(DEC_SEQ, 1, MOBA_BLOCK * B_HEADS)
    bias_prev = jnp.broadcast_to(bias_prev, (DEC_SEQ, B_HEADS, MOBA_BLOCK * B_HEADS))
    bias_prev = bias_prev.reshape(SAMPLE_ROWS, MOBA_BLOCK * B_HEADS)
    bias_new = jnp.transpose(tiles[:, 0, :DEC_SEQ, :DEC_SEQ], (2, 1, 0)).reshape(DEC_SEQ, 1, SAMPLE_ROWS)
    bias_new = jnp.broadcast_to(bias_new, (DEC_SEQ, B_HEADS, SAMPLE_ROWS)).reshape(SAMPLE_ROWS, SAMPLE_ROWS)
    far_col = jnp.tile(rel_bias[REL_BUCKETS - 1], DEC_SEQ).reshape(SAMPLE_ROWS, 1)

    kv_prompt = ()
    ks_l, vs_l = [], []
    cp_l, hp_l, cs_l, hs_l = [], [], [], []
    for i in range(DEPTH):
        j = i // 2
        g_mix = norm_mix[i].reshape(1, D_MODEL)
        g_mlp = norm_mlp[i].reshape(1, D_MODEL)
        final = i == DEPTH - 1
        if i % 2 == 0:
            lru_w = (wcat_bf[j], conv_w[j], conv_b[j].reshape(1, W_A), gate_a_b[j].reshape(1, W_A),
                     gate_x_b[j].reshape(1, W_A), lru_lambda[j].reshape(1, W_A))
            w_out = w_out_a_bf

            def outs_a(qx_kind, qx_dtype):
                return ((0, W_A, "tok", F32), (W_A, W_A, "tok", F32), (2 * W_A, W_X, qx_kind, qx_dtype))

            u, gate, qx = _norm_proj(xp, g_mix, w_in_a_bf, j, outs_a("tok", BF16))
            y_p, utail, htail = _lru_prompt(u, gate, *lru_w)
            cp_l.append(utail[:, SUBLANES - (CONV_WIDTH - 1):, :])
            hp_l.append(htail[:, SUBLANES - 1, :])

            u, gate, qx_s = _norm_proj(xs, g_mix, w_in_a_bf, j, outs_a("rows", F32))
            prev = jnp.pad(state_conv[j], ((0, 0), (SUBLANES - (CONV_WIDTH - 1), 0), (0, 0)))
            h0pad = jnp.pad(state_h[j][:, None, :], ((0, 0), (0, DEC_SEQ - 1), (0, 0)))
            y_s, h_s = _lru_sample(u, gate, prev.reshape(n_sample, W_A), h0pad.reshape(n_sample, W_A), *lru_w)
            cs_l.append(u.reshape(DEC_BATCH, DEC_SEQ, W_A)[:, DEC_SEQ - (CONV_WIDTH - 1):, :])
            hs_l.append(h_s.reshape(DEC_BATCH, DEC_SEQ, W_A)[:, DEC_SEQ - 1, :])
        else:
            w_out = w_out_b_bf

            rows_kind = "rows2" if kv_prompt else "rows"
            outs_p = ((0, W_B, "tok", F32),
                      (W_B, W_B, rows_kind, F32), (W_B, W_B, "tok", BF16), (W_B, W_B, "blockmean", F32),
                      (2 * W_B, W_B, rows_kind, F32), (2 * W_B, W_B, "tok", BF16),
                      (3 * W_B, W_X, "tok", BF16))
            q, k_rows, k_bf, kmean, v_rows, v_bf, qx = _norm_proj(xp, g_mix, w_in_b_bf, j, outs_p, kv_prompt)
            y_p = _moba_prompt(q, k_bf, v_bf, kmean.reshape(BATCH, N_BLOCKS, W_B), tiles, rel_bias)
            kv_prompt = (k_rows, v_rows)

            outs_s = ((0, W_B, "rows", F32), (W_B, W_B, "rows", F32), (2 * W_B, W_B, "rows", F32),
                      (3 * W_B, W_X, "rows", F32))
            q_rows, k_rows, v_rows, qx_s = _norm_proj(xs, g_mix, w_in_b_bf, j, outs_s)
            y_s = _moba_sample(q_rows, k_rows, v_rows, cache_k_rows, cache_v_rows, j, pt_flat,
                               bias_prev, bias_new, far_col).reshape(n_sample, W_B)
            ks_l.append(k_rows.reshape(DEC_BATCH, DEC_SEQ, B_HEADS, B_HEAD_DIM))
            vs_l.append(v_rows.reshape(DEC_BATCH, DEC_SEQ, B_HEADS, B_HEAD_DIM))

        ca_s = _xattn_sample(qx_s, cmk, cmv, i).reshape(n_sample, W_X)
        post_w = (w_out, j, g_mlp, w_up_bf, w_down_bf, i, gf, final)
        xp = _post(xp, y_p, (qx, memk, memv), *post_w)
        xs = _post(xs, y_s, ca_s, *post_w)

    mem_shape = (DEPTH, BATCH, MEM_LEN, X_HEADS, X_HEAD_DIM)
    kv_shape = (DEPTH // 2, BATCH, SEQ, B_HEADS, B_HEAD_DIM)
    return (xp.reshape(BATCH, SEQ, D_MODEL), xs.reshape(DEC_BATCH, DEC_SEQ, D_MODEL),
            kv_prompt[0].reshape(kv_shape), kv_prompt[1].reshape(kv_shape), jnp.stack(ks_l), jnp.stack(vs_l),
            jnp.stack(cp_l), jnp.stack(hp_l), jnp.stack(cs_l), jnp.stack(hs_l),
            memk_rows.reshape(mem_shape), memv_rows.reshape(mem_shape))
```

```python
import functools
import math

import jax
import jax.numpy as jnp
from jax import lax
from jax.experimental import pallas as pl
from jax.experimental.pallas import tpu as pltpu

F32 = jnp.float32
BF16 = jnp.bfloat16

D_MODEL = 1024
BATCH = 8
SEQ = 2048
DEPTH = 4
DEC_BATCH = 128
DEC_SEQ = 8
PAST_LEN = 2048
PAGE_SIZE = 128
N_PAGES = PAST_LEN // PAGE_SIZE
W_A = D_MODEL
LRU_BLOCKS = 8
LRU_BLOCK_DIM = W_A // LRU_BLOCKS
CONV_WIDTH = 4
LRU_C = 8.0
B_HEADS = 8
B_HEAD_DIM = D_MODEL // B_HEADS
W_B = B_HEADS * B_HEAD_DIM
MOBA_BLOCK = 256
MOBA_TOPK = 3
N_BLOCKS = SEQ // MOBA_BLOCK
REL_BUCKETS = 32
REL_MAX_DIST = 128
X_HEADS = 4
X_HEAD_DIM = 128
W_X = X_HEADS * X_HEAD_DIM
MEM_LEN = 256
D_FF = 4 * D_MODEL
EPS = 1e-6

SUBLANES = 8
VMEM_LIMIT = 56 * 1024 * 1024
NEG_INF = float("-inf")


def _cparams(*sem):
    return pltpu.CompilerParams(dimension_semantics=sem, vmem_limit_bytes=VMEM_LIMIT)


def _dot(a, b):
    return jnp.dot(a, b, preferred_element_type=F32)


def _dot_nt(a, b, precision=None):
    return lax.dot_general(a, b, (((1,), (1,)), ((), ())), preferred_element_type=F32, precision=precision)


def _rms(x, g):
    return x * lax.rsqrt(jnp.mean(x * x, axis=-1, keepdims=True) + EPS) * g


def _t5_thresholds():
    max_exact = REL_BUCKETS // 2
    thr = []
    for k in range(1, REL_BUCKETS - max_exact):
        d = max_exact
        while int(math.log(d / max_exact) / math.log(REL_MAX_DIST / max_exact) * (REL_BUCKETS - max_exact)) < k:
            d += 1
        thr.append(d)
    return thr


T5_THRESHOLDS = _t5_thresholds()


def _mem_kv_kernel(x_ref, g_ref, w_ref, k_rows_ref, v_rows_ref, k_bf_ref, v_bf_ref):
    tm = x_ref.shape[0]
    hn = _rms(x_ref[...], g_ref[...]).astype(BF16)
    r = _dot(hn, w_ref[...])
    for rows_ref, bf_ref, off in ((k_rows_ref, k_bf_ref, 0), (v_rows_ref, v_bf_ref, W_X)):
        bf_ref[...] = r[:, off:off + W_X].astype(BF16)
        for h in range(X_HEADS):
            rows_ref[pl.ds(h, tm, stride=X_HEADS), :] = r[:, off + h * X_HEAD_DIM:off + (h + 1) * X_HEAD_DIM]


def _mem_kv(mem2d, norm_mem, w_bf):
    m = mem2d.shape[0]
    tm = 512
    rows_spec = pl.BlockSpec((None, tm * X_HEADS, X_HEAD_DIM), lambda l, i: (l, i, 0))
    tok_spec = pl.BlockSpec((None, tm, W_X), lambda l, i: (l, i, 0))
    rows_shape = jax.ShapeDtypeStruct((DEPTH, m * X_HEADS, X_HEAD_DIM), F32)
    tok_shape = jax.ShapeDtypeStruct((DEPTH, m, W_X), BF16)
    return pl.pallas_call(
        _mem_kv_kernel,
        grid=(DEPTH, m // tm),
        in_specs=[
            pl.BlockSpec((tm, D_MODEL), lambda l, i: (i, 0)),
            pl.BlockSpec((None, 1, D_MODEL), lambda l, i: (l, 0, 0)),
            pl.BlockSpec((None, D_MODEL, 2 * W_X), lambda l, i: (l, 0, 0)),
        ],
        out_specs=[rows_spec, rows_spec, tok_spec, tok_spec],
        out_shape=[rows_shape, rows_shape, tok_shape, tok_shape],
        compiler_params=_cparams("parallel", "parallel"),
        name="mem_kv",
    )(mem2d, norm_mem, w_bf)


LANES = 128


def _norm_proj_kernel(x_ref, g_ref, w_ref, *refs, outs):
    n_earlier = sum(kind == "rows2" for _, _, kind, _ in outs)
    earlier_refs = list(refs[:n_earlier])
    out_refs = refs[n_earlier:]
    tm = x_ref.shape[0]
    for c in range(tm // MOBA_BLOCK):
        rows = slice(c * MOBA_BLOCK, (c + 1) * MOBA_BLOCK)
        hn = _rms(x_ref[rows, :], g_ref[...]).astype(BF16)
        results = {}
        earlier = iter(earlier_refs)
        for o_ref, (off, n, kind, _) in zip(out_refs, outs):
            if off not in results:
                results[off] = _dot(hn, w_ref[:, off:off + n])
            r = results[off]
            heads = n // LANES
            head_rows = slice(c * MOBA_BLOCK * heads, (c + 1) * MOBA_BLOCK * heads)
            if kind == "tok":
                o_ref[rows, :] = r.astype(o_ref.dtype)
            elif kind == "rows":
                for h in range(heads):
                    o_ref[pl.ds(head_rows.start + h, MOBA_BLOCK, stride=heads), :] = r[:, h * LANES:(h + 1) * LANES]
            elif kind == "rows2":
                o_ref[0, head_rows, :] = next(earlier)[head_rows, :]
                for h in range(heads):
                    o_ref[1, pl.ds(head_rows.start + h, MOBA_BLOCK, stride=heads), :] = (
                        r[:, h * LANES:(h + 1) * LANES])
            else:
                o_ref[c] = jnp.mean(r, axis=0, keepdims=True)


def _norm_proj(x, g, w_bf, mixer_layer, outs, earlier=()):
    m = x.shape[0]
    tm = 512
    n_all = w_bf.shape[2]
    specs, shapes, earlier_specs = [], [], []
    for _, n, kind, dt in outs:
        heads = n // LANES
        if kind == "tok":
            specs.append(pl.BlockSpec((tm, n), lambda i: (i, 0)))
            shapes.append(jax.ShapeDtypeStruct((m, n), dt))
        elif kind == "rows":
            specs.append(pl.BlockSpec((tm * heads, LANES), lambda i: (i, 0)))
            shapes.append(jax.ShapeDtypeStruct((m * heads, LANES), dt))
        elif kind == "rows2":
            earlier_specs.append(pl.BlockSpec((tm * heads, LANES), lambda i: (i, 0)))
            specs.append(pl.BlockSpec((2, tm * heads, LANES), lambda i: (0, i, 0)))
            shapes.append(jax.ShapeDtypeStruct((2, m * heads, LANES), dt))
        else:
            specs.append(pl.BlockSpec((tm // MOBA_BLOCK, 1, n), lambda i: (i, 0, 0)))
            shapes.append(jax.ShapeDtypeStruct((m // MOBA_BLOCK, 1, n), dt))
    return pl.pallas_call(
        functools.partial(_norm_proj_kernel, outs=outs),
        grid=(m // tm,),
        in_specs=[
            pl.BlockSpec((tm, D_MODEL), lambda i: (i, 0)),
            pl.BlockSpec((1, D_MODEL), lambda i: (0, 0)),
            pl.BlockSpec((None, D_MODEL, n_all), lambda i: (mixer_layer, 0, 0), pipeline_mode=pl.Buffered(1)),
        ] + earlier_specs,
        out_specs=specs,
        out_shape=shapes,
        compiler_params=_cparams("parallel"),
        name="norm_proj",
    )(x, g, w_bf, *earlier)


def _softmax_pv(s, v_bf):
    m = jnp.max(s, axis=-1, keepdims=True)
    p = jnp.exp(s - m)
    l = jnp.sum(p, axis=-1, keepdims=True)
    return _dot(p.astype(BF16), v_bf) / l


XQ_ROWS = DEC_SEQ * X_HEADS
XMEM_ROWS = MEM_LEN * X_HEADS


def _xattn_sample_kernel(q_ref, k_ref, v_ref, o_ref, *, group):
    scale = X_HEAD_DIM ** -0.5
    r = lax.broadcasted_iota(jnp.int32, (XQ_ROWS, XMEM_ROWS), 0) % X_HEADS
    c = lax.broadcasted_iota(jnp.int32, (XQ_ROWS, XMEM_ROWS), 1) % X_HEADS
    head_pen = jnp.where(r == c, 0.0, NEG_INF)
    for g in range(group):
        rows = slice(g * XQ_ROWS, (g + 1) * XQ_ROWS)
        s = _dot_nt(q_ref[rows, :].astype(BF16), k_ref[g].astype(BF16)) * scale + head_pen
        o_ref[rows, :] = _softmax_pv(s, v_ref[g].astype(BF16))


def _xattn_sample(q_rows, cmk_rows, cmv_rows, layer):
    group = 8
    n_seq = q_rows.shape[0] // XQ_ROWS
    kv_spec = pl.BlockSpec((None, group, XMEM_ROWS, X_HEAD_DIM), lambda i: (layer, i, 0, 0))
    q_spec = pl.BlockSpec((group * XQ_ROWS, X_HEAD_DIM), lambda i: (i, 0))
    return pl.pallas_call(
        functools.partial(_xattn_sample_kernel, group=group),
        grid=(n_seq // group,),
        in_specs=[q_spec, kv_spec, kv_spec],
        out_specs=q_spec,
        out_shape=jax.ShapeDtypeStruct((n_seq * XQ_ROWS, X_HEAD_DIM), F32),
        compiler_params=_cparams("parallel"),
        name="xattn_sample",
    )(q_rows, cmk_rows, cmv_rows)


POST_ROW_CHUNK = 256


def _post_kernel(x_ref, y_ref, *refs, final_norm, fused_xattn):
    if fused_xattn:
        q_ref, mk_ref, mv_ref = refs[:3]
        refs = refs[3:]
    else:
        c_ref = refs[0]
        refs = refs[1:]
    wy_ref, wc_ref, g_ref, wu_ref, wd_ref, gf_ref, o_ref, hn_scr = refs[:8]
    ca_scr = refs[8] if fused_xattn else None
    f = pl.program_id(1)
    tm = o_ref.shape[0]

    def cross_attention(rows):
        return (ca_scr if fused_xattn else c_ref)[rows, :].astype(BF16)

    @pl.when(f == 0)
    def _():
        if fused_xattn:
            for h in range(X_HEADS):
                sl = slice(h * X_HEAD_DIM, (h + 1) * X_HEAD_DIM)
                s = _dot_nt(q_ref[:, sl], mk_ref[:, sl]) * (X_HEAD_DIM ** -0.5)
                ca_scr[:, sl] = _softmax_pv(s, mv_ref[:, sl]).astype(BF16)
        for c in range(tm // POST_ROW_CHUNK):
            rows = slice(c * POST_ROW_CHUNK, (c + 1) * POST_ROW_CHUNK)
            xn = (x_ref[rows, :] + _dot(y_ref[rows, :].astype(BF16), wy_ref[...])
                  + _dot(cross_attention(rows), wc_ref[...]))
            o_ref[rows, :] = xn
            hn_scr[rows, :] = _rms(xn, g_ref[...]).astype(BF16)

    h = _dot(hn_scr[...], wu_ref[...])
    h = jnp.square(jnp.maximum(h, 0.0)).astype(BF16)
    o_ref[...] += _dot(h, wd_ref[...])

    if final_norm:
        @pl.when(f == pl.num_programs(1) - 1)
        def _():
            o_ref[...] = _rms(o_ref[...], gf_ref[...])


def _post(x, y, ca, w_out, mixer_layer, g, w_up, w_down, layer, gf, final_norm):
    m = x.shape[0]
    tm, tf = 1024, 1024
    mix_w = y.shape[1]
    row = lambda i, f: (i, 0)
    const = lambda i, f: (0, 0)
    fused_xattn = isinstance(ca, tuple)
    if fused_xattn:
        tiles_per_seq = SEQ // tm
        mem_spec = pl.BlockSpec((None, MEM_LEN, W_X), lambda i, f: (layer, i // tiles_per_seq, 0))
        ca_args = ca
        ca_specs = [pl.BlockSpec((tm, W_X), row), mem_spec, mem_spec]
    else:
        ca_args = (ca,)
        ca_specs = [pl.BlockSpec((tm, W_X), row)]
    return pl.pallas_call(
        functools.partial(_post_kernel, final_norm=final_norm, fused_xattn=fused_xattn),
        grid=(m // tm, D_FF // tf),
        in_specs=[
            pl.BlockSpec((tm, D_MODEL), row),
            pl.BlockSpec((tm, mix_w), row),
        ] + ca_specs + [
            pl.BlockSpec((None, mix_w, D_MODEL), lambda i, f: (mixer_layer, 0, 0)),
            pl.BlockSpec((None, W_X, D_MODEL), lambda i, f: (mixer_layer, mix_w // W_X, 0)),
            pl.BlockSpec((1, D_MODEL), const),
            pl.BlockSpec((None, D_MODEL, tf), lambda i, f: (layer, 0, f)),
            pl.BlockSpec((None, tf, D_MODEL), lambda i, f: (layer, f, 0)),
            pl.BlockSpec((1, D_MODEL), const),
        ],
        out_specs=pl.BlockSpec((tm, D_MODEL), row),
        out_shape=jax.ShapeDtypeStruct((m, D_MODEL), F32),
        scratch_shapes=[pltpu.VMEM((tm, D_MODEL), BF16)] + ([pltpu.VMEM((tm, W_X), BF16)] if fused_xattn else []),
        compiler_params=_cparams("parallel", "arbitrary"),
        name="post",
    )(x, y, *ca_args, w_out, w_out, g, w_up, w_down, gf)


def _gelu_tanh(x):
    return x * (0.5 * (1.0 + jnp.tanh(math.sqrt(2.0 / math.pi) * (x + 0.044715 * (x * x * x)))))


def _conv_rows(u, u_prev, cw_ref, cb_ref):
    rows = u.shape[0]
    g = rows // SUBLANES
    u3 = u.reshape(g, SUBLANES, W_A)
    p3 = u_prev.reshape(g, SUBLANES, W_A)
    sub = lax.broadcasted_iota(jnp.int32, (g, SUBLANES, W_A), 1)
    xc = cb_ref[...] + cw_ref[CONV_WIDTH - 1:CONV_WIDTH, :] * u3
    for j in range(1, CONV_WIDTH):
        sh = jnp.where(sub >= j, pltpu.roll(u3, j, axis=1), pltpu.roll(p3, j, axis=1))
        xc = xc + cw_ref[CONV_WIDTH - 1 - j:CONV_WIDTH - j, :] * sh
    return xc.reshape(rows, W_A)


def _lru_coeffs(xc, wcat_ref, gab_ref, gxb_ref, lam_ref, first_row, a_scr, b_scr):
    z = -lam_ref[...]
    softplus = jnp.maximum(z, 0.0) + jnp.log1p(jnp.exp(-jnp.abs(z)))
    for n in range(LRU_BLOCKS):
        sl = slice(n * LRU_BLOCK_DIM, (n + 1) * LRU_BLOCK_DIM)
        xn = xc[:, sl]
        ra = _dot(xn.astype(BF16), wcat_ref[n])
        r = jax.nn.sigmoid(ra[:, :LRU_BLOCK_DIM] + gab_ref[:, sl])
        ig = jax.nn.sigmoid(ra[:, LRU_BLOCK_DIM:] + gxb_ref[:, sl])
        log_a = -LRU_C * r * softplus[:, sl]
        th = jnp.tanh(log_a)
        mult = jnp.sqrt(-2.0 * th / (1.0 - th))
        if first_row is not None:
            mult = jnp.where(first_row, 1.0, mult)
        a_scr[:, sl] = jnp.exp(log_a)
        b_scr[:, sl] = mult * ig * xn


def _scan_within_tiles(a, b):
    rows = a.shape[0]
    g = rows // SUBLANES
    a3 = a.reshape(g, SUBLANES, W_A)
    b3 = b.reshape(g, SUBLANES, W_A)
    sub = lax.broadcasted_iota(jnp.int32, (g, SUBLANES, W_A), 1)
    for s in (1, 2, 4):
        keep = sub >= s
        b3 = jnp.where(keep, a3 * pltpu.roll(b3, s, axis=1) + b3, b3)
        a3 = jnp.where(keep, a3 * pltpu.roll(a3, s, axis=1), a3)
    return a3.reshape(rows, W_A), b3.reshape(rows, W_A)


def _lru_prompt_kernel(u_ref, gate_ref, wcat_ref, cw_ref, cb_ref, gab_ref, gxb_ref, lam_ref,
                       y_ref, utail_ref, htail_ref, tail_scr, carry_scr, a_scr, b_scr, h_scr, *, tile):
    t = pl.program_id(1)

    @pl.when(t == 0)
    def _():
        tail_scr[...] = jnp.zeros_like(tail_scr)
        carry_scr[...] = jnp.zeros_like(carry_scr)

    u = u_ref[...]
    u_prev = jnp.concatenate([tail_scr[...], u[:tile - SUBLANES]], axis=0)
    tail_scr[...] = u[tile - SUBLANES:]
    xc = _conv_rows(u, u_prev, cw_ref, cb_ref)
    row = lax.broadcasted_iota(jnp.int32, (tile, LRU_BLOCK_DIM), 0)
    first_row = jnp.logical_and(row == 0, t == 0)
    _lru_coeffs(xc, wcat_ref, gab_ref, gxb_ref, lam_ref, first_row, a_scr, b_scr)
    a, b = _scan_within_tiles(a_scr[...], b_scr[...])
    a_scr[...] = a
    b_scr[...] = b

    def step(g, carry):
        rows = pl.ds(pl.multiple_of(g * SUBLANES, SUBLANES), SUBLANES)
        h = b_scr[rows, :] + a_scr[rows, :] * carry
        h_scr[rows, :] = h
        return jnp.broadcast_to(h[SUBLANES - 1:SUBLANES, :], (SUBLANES, W_A))

    carry_scr[...] = lax.fori_loop(0, tile // SUBLANES, step, carry_scr[...])
    y_ref[...] = (_gelu_tanh(gate_ref[...]) * h_scr[...]).astype(y_ref.dtype)
    utail_ref[...] = u[tile - SUBLANES:]
    htail_ref[...] = h_scr[tile - SUBLANES:, :]


def _lru_prompt(u, gate, wcat, cw, cb, gab, gxb, lam):
    tile = 256
    nt = SEQ // tile
    row = lambda b, t: (b * nt + t, 0)
    const2 = lambda b, t: (0, 0)
    vec = pl.BlockSpec((1, W_A), const2)
    tail_spec = pl.BlockSpec((None, SUBLANES, W_A), lambda b, t: (b, 0, 0))
    return pl.pallas_call(
        functools.partial(_lru_prompt_kernel, tile=tile),
        grid=(BATCH, nt),
        in_specs=[
            pl.BlockSpec((tile, W_A), row),
            pl.BlockSpec((tile, W_A), row),
            pl.BlockSpec(wcat.shape, lambda b, t: (0, 0, 0)),
            pl.BlockSpec((CONV_WIDTH, W_A), const2),
            vec, vec, vec, vec,
        ],
        out_specs=[pl.BlockSpec((tile, W_A), row), tail_spec, tail_spec],
        out_shape=[
            jax.ShapeDtypeStruct((BATCH * SEQ, W_A), BF16),
            jax.ShapeDtypeStruct((BATCH, SUBLANES, W_A), F32),
            jax.ShapeDtypeStruct((BATCH, SUBLANES, W_A), F32),
        ],
        scratch_shapes=[
            pltpu.VMEM((SUBLANES, W_A), F32),
            pltpu.VMEM((SUBLANES, W_A), F32),
            pltpu.VMEM((tile, W_A), F32),
            pltpu.VMEM((tile, W_A), F32),
            pltpu.VMEM((tile, W_A), F32),
        ],
        compiler_params=_cparams("parallel", "arbitrary"),
        name="lru_prompt",
    )(u, gate, wcat, cw, cb, gab, gxb, lam)


def _lru_sample_kernel(u_ref, gate_ref, prev_ref, h0_ref, wcat_ref, cw_ref, cb_ref, gab_ref, gxb_ref, lam_ref,
                       y_ref, h_ref, a_scr, b_scr):
    xc = _conv_rows(u_ref[...], prev_ref[...], cw_ref, cb_ref)
    _lru_coeffs(xc, wcat_ref, gab_ref, gxb_ref, lam_ref, None, a_scr, b_scr)
    a = a_scr[...]
    _, h = _scan_within_tiles(a, b_scr[...] + a * h0_ref[...])
    h_ref[...] = h
    y_ref[...] = (_gelu_tanh(gate_ref[...]) * h).astype(y_ref.dtype)


def _lru_sample(u, gate, prev, h0pad, wcat, cw, cb, gab, gxb, lam):
    m = u.shape[0]
    tile = 256
    row = lambda i: (i, 0)
    const2 = lambda i: (0, 0)
    vec = pl.BlockSpec((1, W_A), const2)
    blk = pl.BlockSpec((tile, W_A), row)
    return pl.pallas_call(
        _lru_sample_kernel,
        grid=(m // tile,),
        in_specs=[blk, blk, blk, blk,
                  pl.BlockSpec(wcat.shape, lambda i: (0, 0, 0)),
                  pl.BlockSpec((CONV_WIDTH, W_A), const2),
                  vec, vec, vec, vec],
        out_specs=[blk, blk],
        out_shape=[jax.ShapeDtypeStruct((m, W_A), BF16), jax.ShapeDtypeStruct((m, W_A), F32)],
        scratch_shapes=[pltpu.VMEM((tile, W_A), F32), pltpu.VMEM((tile, W_A), F32)],
        compiler_params=_cparams("parallel"),
        name="lru_sample",
    )(u, gate, prev, h0pad, wcat, cw, cb, gab, gxb, lam)


def _bias_tile_kernel(rel_ref, o_ref):
    h = pl.program_id(0)
    w = pl.program_id(1)
    shape = (MOBA_BLOCK, MOBA_BLOCK)
    r = lax.broadcasted_iota(jnp.int32, shape, 0)
    c = lax.broadcasted_iota(jnp.int32, shape, 1)
    dist = jnp.maximum(w * MOBA_BLOCK + c - r, 0)
    large = jnp.full(shape, REL_BUCKETS // 2, jnp.int32)
    for thr in T5_THRESHOLDS:
        large = large + (dist >= thr).astype(jnp.int32)
    bucket = jnp.where(dist < REL_BUCKETS // 2, dist, large)
    out = jnp.zeros(shape, F32)
    for b in range(REL_BUCKETS):
        out = jnp.where(bucket == b, rel_ref[b, h], out)
    o_ref[...] = out


def _bias_tiles(rel_bias):
    return pl.pallas_call(
        _bias_tile_kernel,
        grid=(B_HEADS, 2),
        in_specs=[pl.BlockSpec(memory_space=pltpu.SMEM)],
        out_specs=pl.BlockSpec((None, None, MOBA_BLOCK, MOBA_BLOCK), lambda h, w: (h, w, 0, 0)),
        out_shape=jax.ShapeDtypeStruct((B_HEADS, 2, MOBA_BLOCK, MOBA_BLOCK), F32),
        compiler_params=_cparams("parallel", "parallel"),
        name="bias_tiles",
    )(rel_bias)


def _select_topk(gate, n_past):
    blk = lax.broadcasted_iota(jnp.int32, gate.shape, 1)
    past = blk < n_past
    g = jnp.where(past, gate, NEG_INF)
    rank = jnp.zeros(gate.shape, jnp.int32)
    for m in range(gate.shape[1]):
        gm = g[:, m:m + 1]
        beats = jnp.logical_or(gm > g, jnp.logical_and(gm == g, blk > m))
        rank = rank + beats.astype(jnp.int32)
    return jnp.where(jnp.logical_and(past, rank < MOBA_TOPK), 1.0, 0.0)


def _select_topk_t(gate_t, n_past):
    blk = lax.broadcasted_iota(jnp.int32, gate_t.shape, 0)
    past = blk < n_past
    g = jnp.where(past, gate_t, NEG_INF)
    rank = jnp.zeros(gate_t.shape, jnp.int32)
    for m in range(gate_t.shape[0]):
        gm = g[m:m + 1, :]
        beats = jnp.logical_or(gm > g, jnp.logical_and(gm == g, blk > m))
        rank = rank + beats.astype(jnp.int32)
    return jnp.where(jnp.logical_and(past, rank < MOBA_TOPK), 1.0, 0.0)


LOG2E = math.log2(math.e)


def _moba_prompt_kernel(rel_ref, q_ref, k_ref, v_ref, kmean_ref, bias_ref, o_ref, vt_scr, s_scr, bias_scr):
    h = pl.program_id(1)
    scale = B_HEAD_DIM ** -0.5

    for n in range(N_BLOCKS):
        cols = slice(n * MOBA_BLOCK, (n + 1) * MOBA_BLOCK)
        vt_scr[:, cols] = v_ref[cols, :].astype(F32).T.astype(BF16)

    gate_all = _dot_nt(kmean_ref[...], q_ref[...], precision=lax.Precision.HIGHEST)
    query_block = lax.broadcasted_iota(jnp.int32, gate_all.shape, 1) // MOBA_BLOCK
    penalty_all = jnp.where(_select_topk_t(gate_all, query_block) > 0.0, 0.0, NEG_INF)
    far_bias = rel_ref[REL_BUCKETS - 1, h] * LOG2E
    key = lax.broadcasted_iota(jnp.int32, (MOBA_BLOCK, MOBA_BLOCK), 0)
    query = lax.broadcasted_iota(jnp.int32, (MOBA_BLOCK, MOBA_BLOCK), 1)
    bias_scr[0] = jnp.where(key <= query, bias_ref[0] * LOG2E, NEG_INF)
    bias_scr[1] = bias_ref[1] * LOG2E

    def tile_rows(x, op):
        return op(x.reshape(MOBA_BLOCK // SUBLANES, SUBLANES, MOBA_BLOCK), axis=0)

    def attend(own):
        q_rows = slice(own * MOBA_BLOCK, (own + 1) * MOBA_BLOCK)
        qb = q_ref[q_rows, :].astype(BF16)
        penalty_t = penalty_all[:, q_rows]
        first_tile = own * (own + 1) // 2
        m_part = None
        for n in range(own, -1, -1):
            rows = slice(n * MOBA_BLOCK, (n + 1) * MOBA_BLOCK)
            s = _dot_nt(k_ref[rows, :], qb) * (scale * LOG2E)
            if n == own:
                s = s + bias_scr[0]
            elif n == own - 1:
                s = s + bias_scr[1] + penalty_t[n:n + 1, :]
            else:
                s = s + (far_bias + penalty_t[n:n + 1, :])
            s_scr[first_tile + n] = s
            part = tile_rows(s, jnp.max)
            m_part = part if m_part is None else jnp.maximum(m_part, part)
        m = jnp.max(m_part, axis=0, keepdims=True)
        l_part = jnp.zeros((SUBLANES, MOBA_BLOCK), F32)
        acc = jnp.zeros((B_HEAD_DIM, MOBA_BLOCK), F32)
        for n in range(own + 1):
            rows = slice(n * MOBA_BLOCK, (n + 1) * MOBA_BLOCK)
            p = jnp.exp2(s_scr[first_tile + n] - m)
            l_part = l_part + tile_rows(p, jnp.sum)
            acc = acc + _dot(vt_scr[:, rows], p.astype(BF16))
        l = jnp.sum(l_part, axis=0, keepdims=True)
        o_ref[q_rows, :] = (acc / l).T.astype(o_ref.dtype)

    for pair in range(N_BLOCKS // 2):
        attend(N_BLOCKS - 1 - pair)
        attend(pair)


def _moba_prompt(q, k_bf, v_bf, kmean, bias_tiles, rel_bias):
    batch = q.shape[0] // SEQ
    seq_spec = pl.BlockSpec((SEQ, B_HEAD_DIM), lambda b, h: (b, h))
    n_tiles = N_BLOCKS * (N_BLOCKS + 1) // 2
    return pl.pallas_call(
        _moba_prompt_kernel,
        grid=(batch, B_HEADS),
        in_specs=[
            pl.BlockSpec(memory_space=pltpu.SMEM),
            seq_spec, seq_spec, seq_spec,
            pl.BlockSpec((None, N_BLOCKS, B_HEAD_DIM), lambda b, h: (b, 0, h)),
            pl.BlockSpec((None, 2, MOBA_BLOCK, MOBA_BLOCK), lambda b, h: (h, 0, 0, 0)),
        ],
        out_specs=seq_spec,
        out_shape=jax.ShapeDtypeStruct((batch * SEQ, W_B), BF16),
        scratch_shapes=[pltpu.VMEM((B_HEAD_DIM, SEQ), BF16),
                        pltpu.VMEM((n_tiles, MOBA_BLOCK, MOBA_BLOCK), F32),
                        pltpu.VMEM((2, MOBA_BLOCK, MOBA_BLOCK), F32)],
        compiler_params=_cparams("parallel", "parallel"),
        name="moba_prompt",
    )(rel_bias, q, k_bf, v_bf, kmean, bias_tiles)


SAMPLE_ROWS = DEC_SEQ * B_HEADS
PAGE_ROWS = PAGE_SIZE * B_HEADS


def _moba_sample_kernel(pt_ref, q_ref, kn_ref, vn_ref, bprev_ref, bnew_ref, far_ref, *refs):
    del pt_ref
    k_pages = refs[:N_PAGES]
    v_pages = refs[N_PAGES:2 * N_PAGES]
    o_ref = refs[2 * N_PAGES]
    s_scr = refs[2 * N_PAGES + 1]
    scale = B_HEAD_DIM ** -0.5
    pages_per_block = MOBA_BLOCK // PAGE_SIZE
    n_past = PAST_LEN // MOBA_BLOCK

    q = q_ref[...]
    qb = q.astype(BF16)
    q3 = q.reshape(DEC_SEQ, B_HEADS, B_HEAD_DIM)
    gate_cols = []
    for n in range(n_past):
        ksum = sum(k_pages[n * pages_per_block + j][...].reshape(PAGE_SIZE, B_HEADS, B_HEAD_DIM).sum(axis=0)
                   for j in range(pages_per_block))
        kmean = ksum * (1.0 / MOBA_BLOCK)
        gate_cols.append(jnp.sum(q3 * kmean[None], axis=-1, keepdims=True).reshape(SAMPLE_ROWS, 1))
    gate = jnp.concatenate(gate_cols, axis=1)
    penalty = jnp.where(_select_topk(gate, n_past) > 0.0, 0.0, NEG_INF)

    def same_head_penalty(cols):
        r = lax.broadcasted_iota(jnp.int32, (SAMPLE_ROWS, cols), 0) % B_HEADS
        c = lax.broadcasted_iota(jnp.int32, (SAMPLE_ROWS, cols), 1) % B_HEADS
        return jnp.where(r == c, 0.0, NEG_INF)

    head_pen = same_head_penalty(PAGE_ROWS)
    far_bias = far_ref[...]
    m_el = jnp.full((SAMPLE_ROWS, PAGE_ROWS), NEG_INF, F32)
    for p in range(N_PAGES):
        n = p // pages_per_block
        s = _dot_nt(qb, k_pages[p][...].astype(BF16)) * scale
        if n == n_past - 1:
            j = p - n * pages_per_block
            s = s + bprev_ref[:, j * PAGE_ROWS:(j + 1) * PAGE_ROWS] + penalty[:, n:n + 1]
        else:
            s = s + (far_bias + penalty[:, n:n + 1])
        s = s + head_pen
        s_scr[p] = s
        m_el = jnp.maximum(m_el, s)

    s_new = _dot_nt(q, kn_ref[...]) * scale + bnew_ref[...] + same_head_penalty(SAMPLE_ROWS)
    t_q = lax.broadcasted_iota(jnp.int32, (SAMPLE_ROWS, SAMPLE_ROWS), 0) // B_HEADS
    t_k = lax.broadcasted_iota(jnp.int32, (SAMPLE_ROWS, SAMPLE_ROWS), 1) // B_HEADS
    s_new = jnp.where(t_k <= t_q, s_new, NEG_INF)

    m = jnp.maximum(jnp.max(m_el, axis=-1, keepdims=True), jnp.max(s_new, axis=-1, keepdims=True))
    p_new = jnp.exp(s_new - m)
    acc = _dot(p_new, vn_ref[...])
    l_el = jnp.zeros((SAMPLE_ROWS, PAGE_ROWS), F32)
    for p in range(N_PAGES):
        pr = jnp.exp(s_scr[p] - m)
        l_el = l_el + pr
        acc = acc + _dot(pr.astype(BF16), v_pages[p][...].astype(BF16))
    l = jnp.sum(p_new, axis=-1, keepdims=True) + jnp.sum(l_el, axis=-1, keepdims=True)
    o_ref[...] = acc / l


def _moba_sample(q_rows, k_rows, v_rows, cache_k_rows, cache_v_rows, layer, page_table_flat,
                 bias_prev, bias_new, far_col):
    n_seq = q_rows.shape[0] // SAMPLE_ROWS
    tok_spec = pl.BlockSpec((SAMPLE_ROWS, B_HEAD_DIM), lambda b, pt: (b, 0))

    def page_spec(p):
        return pl.BlockSpec((None, None, PAGE_ROWS, B_HEAD_DIM),
                            lambda b, pt: (layer, pt[b * N_PAGES + p], 0, 0))

    def const_spec(a):
        return pl.BlockSpec(a.shape, lambda b, pt: (0, 0))

    pages = [page_spec(p) for p in range(N_PAGES)]
    grid_spec = pltpu.PrefetchScalarGridSpec(
        num_scalar_prefetch=1,
        grid=(n_seq,),
        in_specs=[tok_spec, tok_spec, tok_spec, const_spec(bias_prev), const_spec(bias_new),
                  const_spec(far_col)] + pages + pages,
        out_specs=tok_spec,
        scratch_shapes=[pltpu.VMEM((N_PAGES, SAMPLE_ROWS, PAGE_ROWS), F32)],
    )
    return pl.pallas_call(
        _moba_sample_kernel,
        grid_spec=grid_spec,
        out_shape=jax.ShapeDtypeStruct((n_seq * SAMPLE_ROWS, B_HEAD_DIM), F32),
        compiler_params=_cparams("parallel"),
        name="moba_sample",
    )(page_table_flat, q_rows, k_rows, v_rows, bias_prev, bias_new, far_col,
      *([cache_k_rows] * N_PAGES), *([cache_v_rows] * N_PAGES))


def kernel(x_prompt, x_sample, cache_k, cache_v, state_conv, state_h, cache_mem_k, cache_mem_v, page_table,
           mem_prompt, norm_mix, norm_mlp, norm_mem, norm_final, w_in_a, w_out_a, conv_w, conv_b, gate_a_w,
           gate_a_b, gate_x_w, gate_x_b, lru_lambda, w_in_b, w_out_b, rel_bias, w_mem_kv, w_up, w_down):
    n_prompt = BATCH * SEQ
    n_sample = DEC_BATCH * DEC_SEQ
    xp = x_prompt.reshape(n_prompt, D_MODEL)
    xs = x_sample.reshape(n_sample, D_MODEL)

    w_in_a_bf, w_in_b_bf = w_in_a.astype(BF16), w_in_b.astype(BF16)
    w_out_a_bf, w_out_b_bf = w_out_a.astype(BF16), w_out_b.astype(BF16)
    w_up_bf, w_down_bf = w_up.astype(BF16), w_down.astype(BF16)
    wcat_bf = jnp.concatenate([gate_a_w, gate_x_w], axis=-1).astype(BF16)
    gf = norm_final.reshape(1, D_MODEL)

    memk_rows, memv_rows, memk, memv = _mem_kv(mem_prompt.reshape(BATCH * MEM_LEN, D_MODEL), norm_mem.reshape(DEPTH, 1, D_MODEL),
                         w_mem_kv.astype(BF16))
    cmk = cache_mem_k.reshape(DEPTH, DEC_BATCH, XMEM_ROWS, X_HEAD_DIM)
    cmv = cache_mem_v.reshape(DEPTH, DEC_BATCH, XMEM_ROWS, X_HEAD_DIM)
    cache_k_rows = cache_k.reshape(cache_k.shape[0], cache_k.shape[1], PAGE_ROWS, B_HEAD_DIM)
    cache_v_rows = cache_v.reshape(cache_v.shape[0], cache_v.shape[1], PAGE_ROWS, B_HEAD_DIM)
    pt_flat = page_table.reshape(DEC_BATCH * N_PAGES)

    tiles = _bias_tiles(rel_bias)
    bias_prev = jnp.transpose(tiles[:, 1, :, :DEC_SEQ], (2, 1, 0)).reshape(DEC_SEQ, 1, MOBA_BLOCK * B_HEADS)
    bias_prev = jnp.broadcast_to(bias_prev, (DEC_SEQ, B_HEADS, MOBA_BLOCK * B_HEADS))
    bias_prev = bias_prev.reshape(SAMPLE_ROWS, MOBA_BLOCK * B_HEADS)
    bias_new = jnp.transpose(tiles[:, 0, :DEC_SEQ, :DEC_SEQ], (2, 1, 0)).reshape(DEC_SEQ, 1, SAMPLE_ROWS)
    bias_new = jnp.broadcast_to(bias_new, (DEC_SEQ, B_HEADS, SAMPLE_ROWS)).reshape(SAMPLE_ROWS, SAMPLE_ROWS)
    far_col = jnp.tile(rel_bias[REL_BUCKETS - 1], DEC_SEQ).reshape(SAMPLE_ROWS, 1)

    kv_prompt = ()
    ks_l, vs_l = [], []
    cp_l, hp_l, cs_l, hs_l = [], [], [], []
    for i in range(DEPTH):
        j = i // 2
        g_mix = norm_mix[i].reshape(1, D_MODEL)
        g_mlp = norm_mlp[i].reshape(1, D_MODEL)
        final = i == DEPTH - 1
        if i % 2 == 0:
            lru_w = (wcat_bf[j], conv_w[j], conv_b[j].reshape(1, W_A), gate_a_b[j].reshape(1, W_A),
                     gate_x_b[j].reshape(1, W_A), lru_lambda[j].reshape(1, W_A))
            w_out = w_out_a_bf

            def outs_a(qx_kind, qx_dtype):
                return ((0, W_A, "tok", F32), (W_A, W_A, "tok", F32), (2 * W_A, W_X, qx_kind, qx_dtype))

            u, gate, qx = _norm_proj(xp, g_mix, w_in_a_bf, j, outs_a("tok", BF16))
            y_p, utail, htail = _lru_prompt(u, gate, *lru_w)
            cp_l.append(utail[:, SUBLANES - (CONV_WIDTH - 1):, :])
            hp_l.append(htail[:, SUBLANES - 1, :])

            u, gate, qx_s = _norm_proj(xs, g_mix, w_in_a_bf, j, outs_a("rows", F32))
            prev = jnp.pad(state_conv[j], ((0, 0), (SUBLANES - (CONV_WIDTH - 1), 0), (0, 0)))
            h0pad = jnp.pad(state_h[j][:, None, :], ((0, 0), (0, DEC_SEQ - 1), (0, 0)))
            y_s, h_s = _lru_sample(u, gate, prev.reshape(n_sample, W_A), h0pad.reshape(n_sample, W_A), *lru_w)
            cs_l.append(u.reshape(DEC_BATCH, DEC_SEQ, W_A)[:, DEC_SEQ - (CONV_WIDTH - 1):, :])
            hs_l.append(h_s.reshape(DEC_BATCH, DEC_SEQ, W_A)[:, DEC_SEQ - 1, :])
        else:
            w_out = w_out_b_bf

            rows_kind = "rows2" if kv_prompt else "rows"
            outs_p = ((0, W_B, "tok", F32),
                      (W_B, W_B, rows_kind, F32), (W_B, W_B, "tok", BF16), (W_B, W_B, "blockmean", F32),
                      (2 * W_B, W_B, rows_kind, F32), (2 * W_B, W_B, "tok", BF16),
                      (3 * W_B, W_X, "tok", BF16))
            q, k_rows, k_bf, kmean, v_rows, v_bf, qx = _norm_proj(xp, g_mix, w_in_b_bf, j, outs_p, kv_prompt)
            y_p = _moba_prompt(q, k_bf, v_bf, kmean.reshape(BATCH, N_BLOCKS, W_B), tiles, rel_bias)
            kv_prompt = (k_rows, v_rows)

            outs_s = ((0, W_B, "rows", F32), (W_B, W_B, "rows", F32), (2 * W_B, W_B, "rows", F32),
                      (3 * W_B, W_X, "rows", F32))
            q_rows, k_rows, v_rows, qx_s = _norm_proj(xs, g_mix, w_in_b_bf, j, outs_s)
            y_s = _moba_sample(q_rows, k_rows, v_rows, cache_k_rows, cache_v_rows, j, pt_flat,
                               bias_prev, bias_new, far_col).reshape(n_sample, W_B)
            ks_l.append(k_rows.reshape(DEC_BATCH, DEC_SEQ, B_HEADS, B_HEAD_DIM))
            vs_l.append(v_rows.reshape(DEC_BATCH, DEC_SEQ, B_HEADS, B_HEAD_DIM))

        ca_s = _xattn_sample(qx_s, cmk, cmv, i).reshape(n_sample, W_X)
        post_w = (w_out, j, g_mlp, w_up_bf, w_down_bf, i, gf, final)
        xp = _post(xp, y_p, (qx, memk, memv), *post_w)
        xs = _post(xs, y_s, ca_s, *post_w)

    mem_shape = (DEPTH, BATCH, MEM_LEN, X_HEADS, X_HEAD_DIM)
    kv_shape = (DEPTH // 2, BATCH, SEQ, B_HEADS, B_HEAD_DIM)
    return (xp.reshape(BATCH, SEQ, D_MODEL), xs.reshape(DEC_BATCH, DEC_SEQ, D_MODEL),
            kv_prompt[0].reshape(kv_shape), kv_prompt[1].reshape(kv_shape), jnp.stack(ks_l), jnp.stack(vs_l),
            jnp.stack(cp_l), jnp.stack(hp_l), jnp.stack(cs_l), jnp.stack(hs_l),
            memk_rows.reshape(mem_shape), memv_rows.reshape(mem_shape))
```
